```python
import math
import jax, jax.numpy as jnp
from jax import lax
import numpy as np

D_MODEL = 1024
BATCH = 32
SEQ = 2048
DEPTH = 4

HEAD_DIM = 64
NORM_EPS = 1e-6
NEG = -1e30
N_EVEN = (DEPTH + 1) // 2
N_ODD = DEPTH // 2

A_HEADS = 8
A_PATTERNS = ((128, 1), (512, 4), (2048, 16))
A_WIDTH = A_HEADS * HEAD_DIM

B_HEADS = 8
B_Q_LORA = D_MODEL // 4
B_KV_LORA = D_MODEL // 8
B_NOPE = HEAD_DIM
B_ROPE = HEAD_DIM // 2
B_V = HEAD_DIM
B_QBLOCK = 128
ROPE_THETA = 10000.0
B_WIDTH = B_HEADS * B_V

IN_E_WIDTH = 3 * A_WIDTH + B_Q_LORA + B_KV_LORA + B_ROPE
MIX_E_WIDTH = A_WIDTH + B_WIDTH

C_HEADS = D_MODEL // HEAD_DIM
C_WIDTH = C_HEADS * HEAD_DIM
GRID_W = 64
NA_ROWS = 8
NA_COLS = 16
NA_KEY_COLS = 2 * NA_COLS

D_FF = ((8 * D_MODEL // 3 + 255) // 256) * 256
N_EXPERTS = 8
TOP_K = 2
D_FF_EXPERT = 7 * D_MODEL // 2

kernel_name = "hybrid_dilated_mla_natten_moe_encoder"


def rms_norm(x, g):
    xf = x.astype(jnp.float32)
    y = xf * lax.rsqrt(jnp.mean(xf * xf, axis=-1, keepdims=True) + NORM_EPS)
    return (y * g.astype(jnp.float32)).astype(x.dtype)


def swiglu(x, w_gate, w_up, w_down):
    return (jax.nn.silu(x @ w_gate) * (x @ w_up)) @ w_down


def alibi_slopes(n):
    return jnp.asarray(np.array([2.0 ** (-8.0 * (h + 1) / n) for h in range(n)], np.float32))


def dilated_window_branch(q, k, v, window, dilation, slopes):
    b, s, h, dh = q.shape
    r = window // (2 * dilation)
    L = s // dilation
    n = b * dilation

    def by_residue(t):
        return t.reshape(b, L, dilation, h, dh).transpose(0, 2, 1, 3, 4).reshape(n, L, h, dh)

    qs, ks, vs = by_residue(q), by_residue(k), by_residue(v)
    blk = r
    nb = -(-L // blk)
    lp = nb * blk
    qp = jnp.pad(qs, ((0, 0), (0, lp - L), (0, 0), (0, 0))).reshape(n, nb, blk, h, dh)
    pad_kv = ((0, 0), (blk, lp - L + blk), (0, 0), (0, 0))
    kp = jnp.pad(ks, pad_kv).reshape(n, nb + 2, blk, h, dh)
    vp = jnp.pad(vs, pad_kv).reshape(n, nb + 2, blk, h, dh)

    def band(t):
        return jnp.concatenate([t[:, :-2], t[:, 1:-1], t[:, 2:]], axis=2)

    kb, vb = band(kp), band(vp)
    qi = np.arange(lp).reshape(nb, blk)
    kj = (np.arange(nb)[:, None] * blk - blk) + np.arange(3 * blk)[None]
    rel = kj[:, None, :] - qi[:, :, None]
    valid = (np.abs(rel) <= r) & (kj[:, None, :] >= 0) & (kj[:, None, :] < L)
    dist = jnp.asarray((np.abs(rel) * dilation).astype(np.float32))

    scores = jnp.einsum('nbqhd,nbkhd->nhbqk', qp, kb).astype(jnp.float32) * (dh ** -0.5)
    scores = scores - slopes[None, :, None, None, None] * dist[None, None]
    scores = jnp.where(jnp.asarray(valid)[None, None], scores, NEG)
    lse = jax.nn.logsumexp(scores, axis=-1)
    p = jnp.exp(scores - lse[..., None])
    out = jnp.einsum('nhbqk,nbkhd->nbqhd', p.astype(v.dtype), vb).reshape(n, lp, h, dh)[:, :L]
    lse = lse.transpose(0, 2, 3, 1).reshape(n, lp, h)[:, :L]
    out = out.reshape(b, dilation, L, h, dh).transpose(0, 2, 1, 3, 4).reshape(b, s, h, dh)
    lse = lse.reshape(b, dilation, L, h).transpose(0, 2, 1, 3).reshape(b, s, h)
    return out, lse


def dilated_mixture_attention(q, k, v):
    slopes = alibi_slopes(q.shape[2])
    outs, lses = [], []
    for window, dilation in A_PATTERNS:
        o, l = dilated_window_branch(q, k, v, window, dilation, slopes)
        outs.append(o)
        lses.append(l)
    w = jax.nn.softmax(jnp.stack(lses, axis=0), axis=0)
    out = w[0][..., None].astype(q.dtype) * outs[0]
    for g in range(1, len(outs)):
        out = out + w[g][..., None].astype(q.dtype) * outs[g]
    return out


def rope(x, pos):
    half = x.shape[-1] // 2
    freqs = ROPE_THETA ** (-jnp.arange(half, dtype=jnp.float32) / half)
    ang = pos[:, None].astype(jnp.float32) * freqs[None]
    cos, sin = jnp.cos(ang)[:, None, :], jnp.sin(ang)[:, None, :]
    xf = x.astype(jnp.float32)
    x1, x2 = xf[..., :half], xf[..., half:]
    return jnp.concatenate([x1 * cos - x2 * sin, x1 * sin + x2 * cos], axis=-1).astype(x.dtype)


def latent_attention(q_lat, kv_lat, k_rope, q_norm, w_uq, kv_norm, w_ukv):
    b, s, _ = q_lat.shape
    pos = jnp.arange(s)
    q = (rms_norm(q_lat, q_norm) @ w_uq).reshape(b, s, B_HEADS, B_NOPE + B_ROPE)
    q_nope, q_pe = q[..., :B_NOPE], rope(q[..., B_NOPE:], pos)
    kv = (rms_norm(kv_lat, kv_norm) @ w_ukv).reshape(b, s, B_HEADS, B_NOPE + B_V)
    k_nope, v = kv[..., :B_NOPE], kv[..., B_NOPE:]
    k_pe = rope(k_rope[:, :, None, :], pos)[:, :, 0]
    scale = (B_NOPE + B_ROPE) ** -0.5
    nblk = s // B_QBLOCK
    qn = q_nope.reshape(b, nblk, B_QBLOCK, B_HEADS, B_NOPE).transpose(1, 0, 2, 3, 4)
    qr = q_pe.reshape(b, nblk, B_QBLOCK, B_HEADS, B_ROPE).transpose(1, 0, 2, 3, 4)

    def block(args):
        qn_b, qr_b = args
        sc = (jnp.einsum('bqhd,bkhd->bhqk', qn_b, k_nope)
              + jnp.einsum('bqhr,bkr->bhqk', qr_b, k_pe)).astype(jnp.float32) * scale
        p = jax.nn.softmax(sc, axis=-1)
        return jnp.einsum('bhqk,bkhd->bqhd', p.astype(v.dtype), v)

    out = lax.map(block, (qn, qr))
    return out.transpose(1, 0, 2, 3, 4).reshape(b, s, B_HEADS * B_V)


def neighborhood_attention(q, k, v, rpb):
    b, s, h, dh = q.shape
    rows = s // GRID_W
    kr = min(NA_ROWS, rows)
    n_cb = GRID_W // NA_COLS
    qb = q.reshape(b, rows, n_cb, NA_COLS, h, dh)
    kg = k.reshape(b, rows, GRID_W, h, dh)
    vg = v.reshape(b, rows, GRID_W, h, dh)

    q_c0 = np.arange(n_cb) * NA_COLS
    k_c0 = np.clip(q_c0 - NA_COLS // 2, 0, GRID_W - NA_KEY_COLS)
    key_cols = k_c0[:, None] + np.arange(NA_KEY_COLS)[None]
    q_cols = q_c0[:, None] + np.arange(NA_COLS)[None]
    col_start = np.clip(q_cols - NA_COLS // 2, 0, GRID_W - NA_COLS)
    col_ok = (key_cols[:, None, :] >= col_start[..., None]) & (key_cols[:, None, :] < col_start[..., None] + NA_COLS)
    col_idx = np.clip(key_cols[:, None, :] - q_cols[:, :, None] + NA_COLS - 1, 0, 2 * NA_COLS - 2)
    col_ok = jnp.asarray(col_ok)[None, None, :, :, None, :]
    rpb_f = rpb.astype(jnp.float32)
    scale = dh ** -0.5

    def one_row(i):
        rs = jnp.clip(i - kr // 2, 0, rows - kr)
        k_slab = lax.dynamic_slice_in_dim(kg, rs, kr, axis=1)[:, :, key_cols]
        v_slab = lax.dynamic_slice_in_dim(vg, rs, kr, axis=1)[:, :, key_cols]
        q_row = lax.dynamic_index_in_dim(qb, i, axis=1, keepdims=False)
        sc = jnp.einsum('bcqhd,brckhd->bhcqrk', q_row, k_slab).astype(jnp.float32) * scale
        row_idx = rs + jnp.arange(kr) - i + NA_ROWS - 1
        bias = rpb_f[:, row_idx][:, :, col_idx].transpose(0, 2, 3, 1, 4)
        sc = jnp.where(col_ok, sc + bias[None], NEG)
        p = jax.nn.softmax(sc.reshape(b, h, n_cb, NA_COLS, kr * NA_KEY_COLS), axis=-1)
        p = p.reshape(b, h, n_cb, NA_COLS, kr, NA_KEY_COLS).astype(v.dtype)
        return jnp.einsum('bhcqrk,brckhd->bcqhd', p, v_slab)

    out = lax.map(one_row, jnp.arange(rows))
    return out.transpose(1, 0, 2, 3, 4, 5).reshape(b, s, h * dh)


def moe_swiglu(x, w_router, w_gate, w_up, w_down):
    b, s, d = x.shape
    xt = x.reshape(b * s, d)
    logits = (xt @ w_router).astype(jnp.float32)
    top_val, top_idx = lax.top_k(logits, TOP_K)
    top_w = jax.nn.softmax(top_val, axis=-1)
    gates = jnp.sum(jax.nn.one_hot(top_idx, N_EXPERTS, dtype=jnp.float32) * top_w[..., None], axis=1)
    gates = gates.astype(x.dtype)
    y = jnp.zeros_like(xt)
    for e in range(N_EXPERTS):
        y = y + gates[:, e:e + 1] * swiglu(xt, w_gate[e], w_up[e], w_down[e])
    return y.reshape(b, s, d)


def setup_inputs(seed: int = 0) -> dict:
    key = jax.random.key(seed)
    keys = iter(jax.random.split(key, 32))
    f32 = jnp.float32

    def normal(shape, fan_in):
        return jax.random.normal(next(keys), shape, f32) * (fan_in ** -0.5)

    def gain(shape):
        return 1.0 + 0.05 * jax.random.normal(next(keys), shape, f32)

    return {
        "x": jax.random.normal(next(keys), (BATCH, SEQ, D_MODEL), f32),
        "ln_mix_e": gain((N_EVEN, D_MODEL)),
        "w_in_e": normal((N_EVEN, D_MODEL, IN_E_WIDTH), D_MODEL),
        "mla_q_norm": gain((N_EVEN, B_Q_LORA)),
        "mla_w_uq": normal((N_EVEN, B_Q_LORA, B_HEADS * (B_NOPE + B_ROPE)), B_Q_LORA),
        "mla_kv_norm": gain((N_EVEN, B_KV_LORA)),
        "mla_w_ukv": normal((N_EVEN, B_KV_LORA, B_HEADS * (B_NOPE + B_V)), B_KV_LORA),
        "w_out_e": normal((N_EVEN, MIX_E_WIDTH, D_MODEL), MIX_E_WIDTH),
        "ln_ffn_e": gain((N_EVEN, D_MODEL)),
        "ffn_w_gate": normal((N_EVEN, D_MODEL, D_FF), D_MODEL),
        "ffn_w_up": normal((N_EVEN, D_MODEL, D_FF), D_MODEL),
        "ffn_w_down": normal((N_EVEN, D_FF, D_MODEL), D_FF),
        "ln_mix_o": gain((N_ODD, D_MODEL)),
        "na_w_qkv": normal((N_ODD, D_MODEL, 3 * C_WIDTH), D_MODEL),
        "na_rpb": 0.1 * jax.random.normal(next(keys), (N_ODD, C_HEADS, 2 * NA_ROWS - 1, 2 * NA_COLS - 1), f32),
        "na_w_out": normal((N_ODD, C_WIDTH, D_MODEL), C_WIDTH),
        "ln_ffn_o": gain((N_ODD, D_MODEL)),
        "moe_router": normal((N_ODD, D_MODEL, N_EXPERTS), D_MODEL),
        "moe_w_gate": normal((N_ODD, N_EXPERTS, D_MODEL, D_FF_EXPERT), D_MODEL),
        "moe_w_up": normal((N_ODD, N_EXPERTS, D_MODEL, D_FF_EXPERT), D_MODEL),
        "moe_w_down": normal((N_ODD, N_EXPERTS, D_FF_EXPERT, D_MODEL), D_FF_EXPERT),
        "ln_final": gain((D_MODEL,)),
    }


def reference(x, ln_mix_e, w_in_e, mla_q_norm, mla_w_uq, mla_kv_norm, mla_w_ukv, w_out_e,
              ln_ffn_e, ffn_w_gate, ffn_w_up, ffn_w_down, ln_mix_o, na_w_qkv, na_rpb, na_w_out,
              ln_ffn_o, moe_router, moe_w_gate, moe_w_up, moe_w_down, ln_final):
    b, s, _ = x.shape
    split_e = [A_WIDTH, 2 * A_WIDTH, 3 * A_WIDTH, 3 * A_WIDTH + B_Q_LORA, 3 * A_WIDTH + B_Q_LORA + B_KV_LORA]
    for layer in range(DEPTH):
        if layer % 2 == 0:
            i = layer // 2
            h = rms_norm(x, ln_mix_e[i])
            proj = h @ w_in_e[i]
            qa, ka, va, q_lat, kv_lat, k_rope = jnp.split(proj, split_e, axis=-1)
            qa = qa.reshape(b, s, A_HEADS, HEAD_DIM)
            ka = ka.reshape(b, s, A_HEADS, HEAD_DIM)
            va = va.reshape(b, s, A_HEADS, HEAD_DIM)
            oa = dilated_mixture_attention(qa, ka, va).reshape(b, s, A_WIDTH)
            ob = latent_attention(q_lat, kv_lat, k_rope, mla_q_norm[i], mla_w_uq[i],
                                  mla_kv_norm[i], mla_w_ukv[i])
            x = x + jnp.concatenate([oa, ob], axis=-1) @ w_out_e[i]
            x = x + swiglu(rms_norm(x, ln_ffn_e[i]), ffn_w_gate[i], ffn_w_up[i], ffn_w_down[i])
        else:
            j = layer // 2
            h = rms_norm(x, ln_mix_o[j])
            qc, kc, vc = jnp.split(h @ na_w_qkv[j], 3, axis=-1)
            qc = qc.reshape(b, s, C_HEADS, HEAD_DIM)
            kc = kc.reshape(b, s, C_HEADS, HEAD_DIM)
            vc = vc.reshape(b, s, C_HEADS, HEAD_DIM)
            x = x + neighborhood_attention(qc, kc, vc, na_rpb[j]) @ na_w_out[j]
            x = x + moe_swiglu(rms_norm(x, ln_ffn_o[j]), moe_router[j], moe_w_gate[j],
                               moe_w_up[j], moe_w_down[j])
    return rms_norm(x, ln_final)
```

```python
import functools
import math

import numpy as np
import jax
import jax.numpy as jnp
from jax import lax
from jax.experimental import pallas as pl
from jax.experimental.pallas import tpu as pltpu

F32 = jnp.float32
BF16 = jnp.bfloat16

D_MODEL = 1024
SEQ = 2048
HEAD_DIM = 64
NORM_EPS = 1e-6
NEG = -1e30

A_HEADS = 8
A_WIDTH = A_HEADS * HEAD_DIM
A_PATTERNS = ((128, 1), (512, 4), (2048, 16))

B_HEADS = 8
B_Q_LORA = 256
B_KV_LORA = 128
B_NOPE = 64
B_ROPE = 32
B_V = 64
ROPE_THETA = 10000.0
B_HEAD_LANES = 128

C_HEADS = 16
GRID_W = 64
GRID_ROWS = SEQ // GRID_W
NA_ROWS = 8
NA_COLS = 16

D_FF = 2816
N_EXPERTS = 8
D_FF_EXPERT = 3584

VMEM_LIMIT_BYTES = 56 * 1024 * 1024

TOKEN_TILE = 512
Q_TILE = 256
FFN_CHUNK = 1408
MOE_ROW_TILE = 1024
MOE_F_CHUNK = 512
NA_Q_ROWS = 4
NA_K_ROWS = 12


def _cparams(*sem):
    return pltpu.CompilerParams(dimension_semantics=sem, vmem_limit_bytes=VMEM_LIMIT_BYTES)


def _rms(x, g):
    return x * lax.rsqrt(jnp.mean(x * x, axis=-1, keepdims=True) + NORM_EPS) * g


def _dot(a, b):
    return jnp.dot(a, b, preferred_element_type=F32)


def _dot_nt(a, b):
    return lax.dot_general(a, b, (((1,), (1,)), ((), ())), preferred_element_type=F32)


def _resident(shape):
    zeros = (0,) * len(shape)
    return pl.BlockSpec(shape, lambda *_: zeros, pipeline_mode=pl.Buffered(1))


def _front_even_kernel(x_ref, g_ref, win_ref, qn_ref, kvn_ref, wqa_ref, wqr_ref, wk_ref, wv_ref, place_ref,
                       cq_ref, sq_ref, rk_ref,
                       qa_ref, ka_ref, va_ref, qb_ref, kb_ref, vb_ref):
    h = _rms(x_ref[...], g_ref[...]).astype(BF16)
    proj = _dot(h, win_ref[...])
    qa_ref[...] = proj[:, 0:512].astype(BF16)
    ka_ref[...] = proj[:, 512:1024].astype(BF16)
    va_ref[...] = proj[:, 1024:1536].astype(BF16)

    qn = _rms(proj[:, 1536:1792], qn_ref[...]).astype(BF16)
    q = _dot(qn, wqa_ref[...]) * cq_ref[...] + _dot(qn, wqr_ref[...]) * sq_ref[...]
    qb_ref[...] = q.astype(BF16)

    kvn = _rms(proj[:, 1792:1920], kvn_ref[...]).astype(BF16)
    t = proj[:, 1920:2048] * rk_ref[...]
    k_pe = t + pltpu.roll(t, 96, 1)
    k = _dot(kvn, wk_ref[...]) + _dot(k_pe.astype(BF16), place_ref[...])
    kb_ref[...] = k.astype(BF16)
    vb_ref[...] = _dot(kvn, wv_ref[...]).astype(BF16)


def _front_even(x, g, win, qnorm, kvnorm, wqa, wqr, wk, wv, place, cq, sq, rk, batch):
    n = x.shape[0]
    tm = TOKEN_TILE
    pt = SEQ // tm
    row = lambda p, b: (b * pt + p, 0)
    pos = lambda p, b: (p, 0)
    tok = lambda w: pl.BlockSpec((tm, w), row)
    out = lambda w: jax.ShapeDtypeStruct((n, w), BF16)
    return pl.pallas_call(
        _front_even_kernel,
        grid=(pt, batch),
        in_specs=[tok(D_MODEL), _resident((1, D_MODEL)), _resident(win.shape), _resident((1, B_Q_LORA)),
                  _resident((1, B_KV_LORA)), _resident(wqa.shape), _resident(wqr.shape), _resident(wk.shape),
                  _resident(wv.shape), _resident(place.shape),
                  pl.BlockSpec((tm, 1024), pos), pl.BlockSpec((tm, 1024), pos), pl.BlockSpec((tm, 128), pos)],
        out_specs=[tok(512), tok(512), tok(512), tok(1024), tok(1024), tok(512)],
        out_shape=[out(512), out(512), out(512), out(1024), out(1024), out(512)],
        compiler_params=_cparams("arbitrary", "arbitrary"),
        name="front_even",
    )(x, g, win, qnorm, kvnorm, wqa, wqr, wk, wv, place, cq, sq, rk)


A_TILE_REACH = 4


def _head_lane_mask(shape):
    return lax.broadcasted_iota(jnp.int32, shape, 1) < HEAD_DIM


def _attn_a_kernel(q_ref, k_ref, v_ref, bias_ref, o_ref):
    n_tiles = SEQ // Q_TILE

    def q_tile(i, carry):
        q0 = pl.multiple_of(i * Q_TILE, Q_TILE)
        q = q_ref[pl.ds(q0, Q_TILE), :]
        lo_mask = _head_lane_mask(q.shape)
        outs = []
        for head in range(2):
            qm = jnp.where(lo_mask if head == 0 else jnp.logical_not(lo_mask), q, jnp.zeros_like(q))

            def k_tile(j, state, qm=qm, head=head):
                m, l, acc = state
                k0 = pl.multiple_of(j * Q_TILE, Q_TILE)
                s = _dot_nt(qm, k_ref[pl.ds(k0, Q_TILE), :]) + bias_ref[head, j - i + A_TILE_REACH]
                m_new = jnp.maximum(m, jnp.max(s, axis=-1, keepdims=True))
                alpha = jnp.exp(m - m_new)
                p = jnp.exp(s - m_new)
                l = alpha * l + jnp.sum(p, axis=-1, keepdims=True)
                acc = alpha * acc + _dot(p.astype(BF16), v_ref[pl.ds(k0, Q_TILE), :])
                return m_new, l, acc

            init = (jnp.full((Q_TILE, 1), NEG, F32), jnp.zeros((Q_TILE, 1), F32),
                    jnp.zeros((Q_TILE, 2 * HEAD_DIM), F32))
            j_lo = jnp.maximum(i - A_TILE_REACH, 0)
            j_hi = jnp.minimum(i + A_TILE_REACH + 1, n_tiles)
            _, l, acc = lax.fori_loop(j_lo, j_hi, k_tile, init)
            outs.append(acc / l)
        o_ref[pl.ds(q0, Q_TILE), :] = jnp.where(lo_mask, outs[0], outs[1]).astype(o_ref.dtype)
        return carry

    lax.fori_loop(0, n_tiles, q_tile, 0)


def _attn_a(qa, ka, va, bias, batch):
    n = qa.shape[0]
    blk = pl.BlockSpec((SEQ, 2 * HEAD_DIM), lambda hp, b: (b, hp))
    return pl.pallas_call(
        _attn_a_kernel,
        grid=(A_HEADS // 2, batch),
        in_specs=[blk, blk, blk,
                  pl.BlockSpec((2, 2 * A_TILE_REACH + 1, Q_TILE, Q_TILE), lambda hp, b: (hp, 0, 0, 0))],
        out_specs=blk,
        out_shape=jax.ShapeDtypeStruct((n, A_WIDTH), BF16),
        compiler_params=_cparams("arbitrary", "arbitrary"),
        name="attn_dilated",
    )(qa, ka, va, bias)


def _dilated_bias():
    o = np.arange(-A_TILE_REACH, A_TILE_REACH + 1)[:, None, None]
    d = Q_TILE * o + np.arange(Q_TILE)[None, None, :] - np.arange(Q_TILE)[None, :, None]
    ad = np.abs(d)
    mult = np.zeros(d.shape, np.int32)
    for window, dilation in A_PATTERNS:
        mult += ((d % dilation == 0) & (ad <= window // 2)).astype(np.int32)
    logm = np.where(mult > 0, np.log(np.maximum(mult, 1).astype(np.float64)), 0.0).astype(np.float32)
    slopes = np.array([2.0 ** (-8.0 * (h + 1) / A_HEADS) for h in range(A_HEADS)], np.float32)
    bias = -jnp.asarray(slopes)[:, None, None, None] * jnp.asarray(ad.astype(np.float32))[None] + jnp.asarray(logm)[None]
    return jnp.where(jnp.asarray(mult > 0)[None], bias, NEG)


def _attn_mla_kernel(q_ref, k_ref, v_ref, o_ref):
    def q_tile(i, carry):
        q0 = pl.multiple_of(i * Q_TILE, Q_TILE)
        outs = []
        for head in range(2):
            lanes = slice(head * B_HEAD_LANES, (head + 1) * B_HEAD_LANES)
            s = _dot_nt(q_ref[pl.ds(q0, Q_TILE), lanes], k_ref[:, lanes])
            p = jnp.exp(s - jnp.max(s, axis=-1, keepdims=True))
            l = jnp.sum(p, axis=-1, keepdims=True)
            outs.append(_dot(p.astype(BF16), v_ref[...]) / l)
        lo_mask = _head_lane_mask(outs[0].shape)
        o_ref[pl.ds(q0, Q_TILE), :] = jnp.where(lo_mask, outs[0], outs[1]).astype(o_ref.dtype)
        return carry

    lax.fori_loop(0, SEQ // Q_TILE, q_tile, 0)


def _attn_mla(qb, kb, vb, batch):
    n = qb.shape[0]
    qk = pl.BlockSpec((SEQ, 2 * B_HEAD_LANES), lambda hp, b: (b, hp))
    vo = pl.BlockSpec((SEQ, 2 * B_V), lambda hp, b: (b, hp))
    return pl.pallas_call(
        _attn_mla_kernel,
        grid=(B_HEADS // 2, batch),
        in_specs=[qk, qk, vo],
        out_specs=vo,
        out_shape=jax.ShapeDtypeStruct((n, B_HEADS * B_V), BF16),
        compiler_params=_cparams("arbitrary", "arbitrary"),
        name="attn_latent",
    )(qb, kb, vb)


def _post_even_kernel(x_ref, oa_ref, ob_ref, woa_ref, wob_ref, g_ref, wg_ref, wu_ref, wd_ref, o_ref):
    x1 = x_ref[...] + _dot(oa_ref[...], woa_ref[...]) + _dot(ob_ref[...], wob_ref[...])
    h = _rms(x1, g_ref[...]).astype(BF16)
    y = None
    for c in range(D_FF // FFN_CHUNK):
        cols = slice(c * FFN_CHUNK, (c + 1) * FFN_CHUNK)
        a = jax.nn.silu(_dot(h, wg_ref[:, cols])) * _dot(h, wu_ref[:, cols])
        d = _dot(a.astype(BF16), wd_ref[cols, :])
        y = d if y is None else y + d
    o_ref[...] = x1 + y


def _post_even(x, oa, ob, woa, wob, g, wg, wu, wd):
    n = x.shape[0]
    tm = TOKEN_TILE
    tok = lambda w: pl.BlockSpec((tm, w), lambda i: (i, 0))
    return pl.pallas_call(
        _post_even_kernel,
        grid=(n // tm,),
        in_specs=[tok(D_MODEL), tok(512), tok(512), _resident(woa.shape), _resident(wob.shape),
                  _resident((1, D_MODEL)), _resident(wg.shape), _resident(wu.shape), _resident(wd.shape)],
        out_specs=tok(D_MODEL),
        out_shape=jax.ShapeDtypeStruct((n, D_MODEL), F32),
        compiler_params=_cparams("arbitrary"),
        name="post_even",
    )(x, oa, ob, woa, wob, g, wg, wu, wd)


def _front_odd_kernel(x_ref, g_ref, w_ref, q_ref, k_ref, v_ref):
    h = _rms(x_ref[...], g_ref[...]).astype(BF16)
    qkv = _dot(h, w_ref[...])
    q_ref[...] = qkv[:, 0:1024].astype(BF16)
    k_ref[...] = qkv[:, 1024:2048].astype(BF16)
    v_ref[...] = qkv[:, 2048:3072].astype(BF16)


def _front_odd(x, g, w):
    n = x.shape[0]
    tm = TOKEN_TILE
    tok = pl.BlockSpec((tm, D_MODEL), lambda i: (i, 0))
    out = jax.ShapeDtypeStruct((n, D_MODEL), BF16)
    return pl.pallas_call(
        _front_odd_kernel,
        grid=(n // tm,),
        in_specs=[tok, _resident((1, D_MODEL)), _resident(w.shape)],
        out_specs=[tok, tok, tok],
        out_shape=[out, out, out],
        compiler_params=_cparams("arbitrary"),
        name="front_odd",
    )(x, g, w)


NA_Q = NA_Q_ROWS * GRID_W
NA_K = NA_K_ROWS * GRID_W
NA_BLOCKS = GRID_ROWS // NA_Q_ROWS


def _na_key_row_start(block):
    return np.clip(NA_Q_ROWS * block - NA_ROWS // 2, 0, GRID_ROWS - NA_K_ROWS)


def _attn_na_kernel(q_ref, k_ref, v_ref, bias_ref, o_ref):
    def q_block(bi, carry):
        q0 = pl.multiple_of(bi * NA_Q, NA_Q)
        ks = jnp.clip(NA_Q_ROWS * bi - NA_ROWS // 2, 0, GRID_ROWS - NA_K_ROWS)
        k0 = pl.multiple_of(ks * GRID_W, GRID_W)
        case = jnp.where(bi == 0, 0, jnp.where(bi == NA_BLOCKS - 1, 2, 1))
        q = q_ref[pl.ds(q0, NA_Q), :]
        k = k_ref[pl.ds(k0, NA_K), :]
        v = v_ref[pl.ds(k0, NA_K), :]
        lo_mask = _head_lane_mask(q.shape)
        outs = []
        for head in range(2):
            qm = jnp.where(lo_mask if head == 0 else jnp.logical_not(lo_mask), q, jnp.zeros_like(q))
            s = _dot_nt(qm, k) + bias_ref[head, case]
            p = jnp.exp(s - jnp.max(s, axis=-1, keepdims=True))
            l = jnp.sum(p, axis=-1, keepdims=True)
            outs.append(_dot(p.astype(BF16), v) / l)
        o_ref[pl.ds(q0, NA_Q), :] = jnp.where(lo_mask, outs[0], outs[1]).astype(o_ref.dtype)
        return carry

    lax.fori_loop(0, NA_BLOCKS, q_block, 0)


def _attn_na(q, k, v, bias, batch):
    n = q.shape[0]
    blk = pl.BlockSpec((SEQ, 2 * HEAD_DIM), lambda hp, b: (b, hp))
    return pl.pallas_call(
        _attn_na_kernel,
        grid=(C_HEADS // 2, batch),
        in_specs=[blk, blk, blk, pl.BlockSpec((2, 3, NA_Q, NA_K), lambda hp, b: (hp, 0, 0, 0))],
        out_specs=blk,
        out_shape=jax.ShapeDtypeStruct((n, C_HEADS * HEAD_DIM), BF16),
        compiler_params=_cparams("arbitrary", "arbitrary"),
        name="attn_neighbourhood",
    )(q, k, v, bias)


def _na_bias(rpb):
    row_idx, col_idx, valid = [], [], []
    for block in (0, 1, NA_BLOCKS - 1):
        qi = NA_Q_ROWS * block + np.arange(NA_Q_ROWS)[:, None, None, None]
        qc = np.arange(GRID_W)[None, :, None, None]
        kr = _na_key_row_start(block) + np.arange(NA_K_ROWS)[None, None, :, None]
        kc = np.arange(GRID_W)[None, None, None, :]
        rs = np.clip(qi - NA_ROWS // 2, 0, GRID_ROWS - NA_ROWS)
        cs = np.clip(qc - NA_COLS // 2, 0, GRID_W - NA_COLS)
        ok = (kr >= rs) & (kr < rs + NA_ROWS) & (kc >= cs) & (kc < cs + NA_COLS)
        ri = np.clip(kr - qi + NA_ROWS - 1, 0, 2 * NA_ROWS - 2) + 0 * kc
        ci = np.clip(kc - qc + NA_COLS - 1, 0, 2 * NA_COLS - 2) + 0 * kr
        shape = (NA_Q, NA_K)
        row_idx.append(np.broadcast_to(ri, ok.shape).reshape(shape))
        col_idx.append(np.broadcast_to(ci, ok.shape).reshape(shape))
        valid.append(ok.reshape(shape))
    row_idx, col_idx, valid = np.stack(row_idx), np.stack(col_idx), np.stack(valid)
    table = rpb.astype(F32)[:, row_idx, col_idx]
    return jnp.where(jnp.asarray(valid)[None], table, NEG)


def _post_odd_kernel(x_ref, att_ref, wo_ref, g_ref, wr_ref, x1_ref, xn_ref, idx_ref, gate_ref):
    x1 = x_ref[...] + _dot(att_ref[...], wo_ref[...])
    x1_ref[...] = x1
    xn = _rms(x1, g_ref[...])
    xn_ref[...] = xn
    logits = jnp.dot(xn, wr_ref[...], precision=lax.Precision.HIGHEST, preferred_element_type=F32)
    lane = lax.broadcasted_iota(jnp.int32, logits.shape, 1)
    logits = jnp.where(lane < N_EXPERTS, logits, -jnp.inf)
    m1 = jnp.max(logits, axis=-1, keepdims=True)
    i1 = jnp.min(jnp.where(logits == m1, lane, 128), axis=-1, keepdims=True)
    rest = jnp.where(lane == i1, -jnp.inf, logits)
    m2 = jnp.max(rest, axis=-1, keepdims=True)
    i2 = jnp.min(jnp.where(rest == m2, lane, 128), axis=-1, keepdims=True)
    e2 = jnp.exp(m2 - m1)
    w1 = 1.0 / (1.0 + e2)
    w2 = e2 / (1.0 + e2)
    idx_ref[...] = jnp.where(lane == 0, i1, jnp.where(lane == 1, i2, 0))
    gate_ref[...] = jnp.where(lane == 0, w1, jnp.where(lane == 1, w2, 0.0))


def _post_odd(x, att, wo, g, wr):
    n = x.shape[0]
    tm = TOKEN_TILE
    tok = lambda w: pl.BlockSpec((tm, w), lambda i: (i, 0))
    return pl.pallas_call(
        _post_odd_kernel,
        grid=(n // tm,),
        in_specs=[tok(D_MODEL), tok(D_MODEL), _resident(wo.shape), _resident((1, D_MODEL)), _resident(wr.shape)],
        out_specs=[tok(D_MODEL), tok(D_MODEL), tok(128), tok(128)],
        out_shape=[jax.ShapeDtypeStruct((n, D_MODEL), F32), jax.ShapeDtypeStruct((n, D_MODEL), F32),
                   jax.ShapeDtypeStruct((n, 128), jnp.int32), jax.ShapeDtypeStruct((n, 128), F32)],
        compiler_params=_cparams("arbitrary"),
        name="post_odd_router",
    )(x, att, wo, g, wr)


def _moe_kernel(te_ref, nv_ref, nt_ref, src_ref, dst_ref, xn_hbm, wg_ref, wu_ref, wd_ref, y_hbm,
                xbuf, acc, sem_in, sem_out):
    del te_ref
    j = pl.program_id(0)
    f = pl.program_id(1)
    ts = MOE_ROW_TILE
    active = j < nt_ref[0]

    def row_in(r, tok):
        return pltpu.make_async_copy(xn_hbm.at[pl.ds(tok, 1), :], xbuf.at[pl.ds(r, 1), :], sem_in)

    def row_out(r, dst):
        return pltpu.make_async_copy(acc.at[pl.ds(r, 1), :], y_hbm.at[pl.ds(dst, 1), :], sem_out)

    @pl.when(jnp.logical_and(active, f == 0))
    def _gather():
        def start(r, c):
            row_in(r, src_ref[0, 0, r]).start()
            return c
        lax.fori_loop(0, ts, start, 0)

        def wait(r, c):
            row_in(r, 0).wait()
            return c
        lax.fori_loop(0, ts, wait, 0)
        acc[...] = jnp.zeros_like(acc)

    @pl.when(active)
    def _compute():
        xb = xbuf[...].astype(BF16)
        a = jax.nn.silu(_dot(xb, wg_ref[0])) * _dot(xb, wu_ref[0])
        acc[...] += _dot(a.astype(BF16), wd_ref[0])

    @pl.when(jnp.logical_and(active, f == pl.num_programs(1) - 1))
    def _scatter():
        nv = nv_ref[j]

        def start(r, c):
            row_out(r, dst_ref[0, 0, r]).start()
            return c
        lax.fori_loop(0, nv, start, 0)

        def wait(r, c):
            row_out(r, 0).wait()
            return c
        lax.fori_loop(0, nv, wait, 0)


def _moe(xn, wg, wu, wd, tile_expert, tile_rows, n_tiles, src, dst):
    n = xn.shape[0]
    ts = MOE_ROW_TILE
    fc = MOE_F_CHUNK
    tiles = src.shape[0]
    nf = D_FF_EXPERT // fc

    def f_eff(j, f, nt):
        return jnp.where(j < nt[0], f, nf - 1)

    smem = lambda: pl.BlockSpec((1, 1, ts), lambda j, f, te, nv, nt: (j, 0, 0), memory_space=pltpu.SMEM)
    grid_spec = pltpu.PrefetchScalarGridSpec(
        num_scalar_prefetch=3,
        grid=(tiles, nf),
        in_specs=[smem(), smem(), pl.BlockSpec(memory_space=pl.ANY),
                  pl.BlockSpec((1, D_MODEL, fc), lambda j, f, te, nv, nt: (te[j], 0, f_eff(j, f, nt))),
                  pl.BlockSpec((1, D_MODEL, fc), lambda j, f, te, nv, nt: (te[j], 0, f_eff(j, f, nt))),
                  pl.BlockSpec((1, fc, D_MODEL), lambda j, f, te, nv, nt: (te[j], f_eff(j, f, nt), 0))],
        out_specs=pl.BlockSpec(memory_space=pl.ANY),
        scratch_shapes=[pltpu.VMEM((ts, D_MODEL), F32), pltpu.VMEM((ts, D_MODEL), F32),
                        pltpu.SemaphoreType.DMA, pltpu.SemaphoreType.DMA],
    )
    return pl.pallas_call(
        _moe_kernel,
        grid_spec=grid_spec,
        out_shape=jax.ShapeDtypeStruct((2 * n, D_MODEL), F32),
        compiler_params=_cparams("arbitrary", "arbitrary"),
        name="moe_experts",
    )(tile_expert, tile_rows, n_tiles, src, dst, xn, wg, wu, wd)


def _moe_plan(expert_idx):
    n = expert_idx.shape[0]
    ts = MOE_ROW_TILE
    tiles = 2 * n // ts + N_EXPERTS
    flat = expert_idx.reshape(-1)
    onehot = (flat[:, None] == jnp.arange(N_EXPERTS, dtype=jnp.int32)[None]).astype(jnp.int32)
    before = jnp.cumsum(onehot, axis=0) - onehot
    rank = jnp.sum(before * onehot, axis=1)
    counts = jnp.sum(onehot, axis=0)
    padded = (counts + ts - 1) // ts * ts
    ends = jnp.cumsum(padded)
    starts = ends - padded
    pos = starts[flat] + rank
    a = jnp.arange(2 * n, dtype=jnp.int32)
    token, k = a // 2, a % 2
    src = jnp.zeros((tiles * ts,), jnp.int32).at[pos].set(token)
    dst = jnp.zeros((tiles * ts,), jnp.int32).at[pos].set(k * n + token)
    n_tiles = ends[-1] // ts
    tile_start = jnp.arange(tiles, dtype=jnp.int32) * ts
    tile_expert = jnp.minimum(jnp.searchsorted(ends, tile_start, side="right"), N_EXPERTS - 1).astype(jnp.int32)
    tile_rows = jnp.clip(counts[tile_expert] - (tile_start - starts[tile_expert]), 0, ts).astype(jnp.int32)
    last = tile_expert[jnp.maximum(n_tiles - 1, 0)]
    tile_expert = jnp.where(tile_start < ends[-1], tile_expert, last)
    return (tile_expert, tile_rows, n_tiles.reshape(1).astype(jnp.int32),
            src.reshape(tiles, 1, ts), dst.reshape(tiles, 1, ts))


def _combine_kernel(x_ref, y0_ref, y1_ref, gate_ref, g_ref, o_ref, *, final_norm):
    gates = gate_ref[...]
    y = gates[:, 0:1] * y0_ref[...] + gates[:, 1:2] * y1_ref[...]
    x = x_ref[...] + y
    o_ref[...] = _rms(x, g_ref[...]) if final_norm else x


def _combine(x1, y, gates, g, final_norm):
    n = x1.shape[0]
    tm = TOKEN_TILE
    nb = n // tm
    tok = lambda w: pl.BlockSpec((tm, w), lambda i: (i, 0))
    return pl.pallas_call(
        functools.partial(_combine_kernel, final_norm=final_norm),
        grid=(nb,),
        in_specs=[tok(D_MODEL), tok(D_MODEL), pl.BlockSpec((tm, D_MODEL), lambda i: (nb + i, 0)), tok(128),
                  _resident((1, D_MODEL))],
        out_specs=tok(D_MODEL),
        out_shape=jax.ShapeDtypeStruct((n, D_MODEL), F32),
        compiler_params=_cparams("arbitrary"),
        name="moe_combine",
    )(x1, y, y, gates, g)


def _rope_tables():
    half = B_ROPE // 2
    freqs = ROPE_THETA ** (-jnp.arange(half, dtype=F32) / half)
    ang = jnp.arange(SEQ, dtype=F32)[:, None] * freqs[None]
    cos = jnp.concatenate([jnp.cos(ang), jnp.cos(ang)], axis=-1)
    sin = jnp.concatenate([jnp.sin(ang), jnp.sin(ang)], axis=-1)
    scale = (B_NOPE + B_ROPE) ** -0.5
    ones = jnp.ones((SEQ, B_NOPE), F32)
    zeros_n = jnp.zeros((SEQ, B_NOPE), F32)
    zeros_p = jnp.zeros((SEQ, B_HEAD_LANES - B_NOPE - B_ROPE), F32)
    cq = jnp.tile(jnp.concatenate([ones, cos, zeros_p], axis=-1), (1, B_HEADS)) * scale
    sq = jnp.tile(jnp.concatenate([zeros_n, sin, zeros_p], axis=-1), (1, B_HEADS)) * scale
    rk = jnp.concatenate([cos, sin, jnp.zeros((SEQ, 64), F32)], axis=-1)
    return cq, sq, rk


def _half_swap(w):
    half = B_ROPE // 2
    return jnp.concatenate([-w[..., half:], w[..., :half]], axis=-1)


def _prep_even(w_in, w_uq, w_ukv, w_out):
    scale_a = HEAD_DIM ** -0.5
    k_rope = w_in[:, 3 * A_WIDTH + B_Q_LORA + B_KV_LORA:]
    win = jnp.concatenate([w_in[:, :A_WIDTH] * scale_a, w_in[:, A_WIDTH:3 * A_WIDTH + B_Q_LORA + B_KV_LORA],
                           k_rope, _half_swap(k_rope), jnp.zeros((D_MODEL, 64), F32)], axis=-1).astype(BF16)
    uq = w_uq.reshape(B_Q_LORA, B_HEADS, B_NOPE + B_ROPE)
    pad = jnp.zeros((B_Q_LORA, B_HEADS, B_HEAD_LANES - B_NOPE - B_ROPE), F32)
    wqa = jnp.concatenate([uq, pad], axis=-1).reshape(B_Q_LORA, -1).astype(BF16)
    wqr = jnp.concatenate([jnp.zeros_like(uq[..., :B_NOPE]), _half_swap(uq[..., B_NOPE:]), pad], axis=-1)
    wqr = wqr.reshape(B_Q_LORA, -1).astype(BF16)
    ukv = w_ukv.reshape(B_KV_LORA, B_HEADS, B_NOPE + B_V)
    wk = jnp.concatenate([ukv[..., :B_NOPE], jnp.zeros((B_KV_LORA, B_HEADS, B_HEAD_LANES - B_NOPE), F32)], axis=-1)
    wk = wk.reshape(B_KV_LORA, -1).astype(BF16)
    wv = ukv[..., B_NOPE:].reshape(B_KV_LORA, -1).astype(BF16)
    place = np.zeros((128, B_HEADS, B_HEAD_LANES), np.float32)
    for r in range(B_ROPE):
        place[r, :, B_NOPE + r] = 1.0
    place = jnp.asarray(place.reshape(128, -1), BF16)
    wo = w_out.astype(BF16)
    return win, wqa, wqr, wk, wv, place, wo[:A_WIDTH], wo[A_WIDTH:]


def kernel(x, ln_mix_e, w_in_e, mla_q_norm, mla_w_uq, mla_kv_norm, mla_w_ukv, w_out_e, ln_ffn_e, ffn_w_gate,
           ffn_w_up, ffn_w_down, ln_mix_o, na_w_qkv, na_rpb, na_w_out, ln_ffn_o, moe_router, moe_w_gate,
           moe_w_up, moe_w_down, ln_final):
    batch, seq, d = x.shape
    assert (seq, d) == (SEQ, D_MODEL)
    n = batch * seq
    depth = ln_mix_e.shape[0] + ln_mix_o.shape[0]
    row = lambda v: v.reshape(1, -1).astype(F32)
    cq, sq, rk = _rope_tables()
    bias_a = _dilated_bias()
    xf = x.reshape(n, d)
    for layer in range(depth):
        i = layer // 2
        if layer % 2 == 0:
            win, wqa, wqr, wk, wv, place, woa, wob = _prep_even(w_in_e[i], mla_w_uq[i], mla_w_ukv[i], w_out_e[i])
            qa, ka, va, qb, kb, vb = _front_even(xf, row(ln_mix_e[i]), win, row(mla_q_norm[i]), row(mla_kv_norm[i]),
                                                 wqa, wqr, wk, wv, place, cq, sq, rk, batch)
            oa = _attn_a(qa, ka, va, bias_a, batch)
            ob = _attn_mla(qb, kb, vb, batch)
            xf = _post_even(xf, oa, ob, woa, wob, row(ln_ffn_e[i]), ffn_w_gate[i].astype(BF16),
                            ffn_w_up[i].astype(BF16), ffn_w_down[i].astype(BF16))
        else:
            wqkv = jnp.concatenate([na_w_qkv[i][:, :D_MODEL] * HEAD_DIM ** -0.5, na_w_qkv[i][:, D_MODEL:]], axis=-1)
            q, k, v = _front_odd(xf, row(ln_mix_o[i]), wqkv.astype(BF16))
            att = _attn_na(q, k, v, _na_bias(na_rpb[i]), batch)
            wr = jnp.concatenate([moe_router[i].astype(F32), jnp.zeros((D_MODEL, 128 - N_EXPERTS), F32)], axis=-1)
            x1, xn, idx, gates = _post_odd(xf, att, na_w_out[i].astype(BF16), row(ln_ffn_o[i]), wr)
            plan = _moe_plan(idx[:, :2])
            y = _moe(xn, moe_w_gate[i].astype(BF16), moe_w_up[i].astype(BF16), moe_w_down[i].astype(BF16), *plan)
            last = layer == depth - 1
            xf = _combine(x1, y, gates, row(ln_final), final_norm=last)
    if depth % 2 == 1:
        raise NotImplementedError("final norm is fused into the last odd layer")
    return xf.reshape(batch, seq, d)
```

```python
import functools
import math

import numpy as np
import jax
import jax.numpy as jnp
from jax import lax
from jax.experimental import pallas as pl
from jax.experimental.pallas import tpu as pltpu

F32 = jnp.float32
BF16 = jnp.bfloat16

D_MODEL = 1024
SEQ = 2048
HEAD_DIM = 64
NORM_EPS = 1e-6
NEG = -1e30

A_HEADS = 8
A_WIDTH = A_HEADS * HEAD_DIM
A_PATTERNS = ((128, 1), (512, 4), (2048, 16))

B_HEADS = 8
B_Q_LORA = 256
B_KV_LORA = 128
B_NOPE = 64
B_ROPE = 32
B_V = 64
ROPE_THETA = 10000.0
B_HEAD_LANES = 128

C_HEADS = 16
GRID_W = 64
GRID_ROWS = SEQ // GRID_W
NA_ROWS = 8
NA_COLS = 16

D_FF = 2816
N_EXPERTS = 8
D_FF_EXPERT = 3584

VMEM_LIMIT_BYTES = 56 * 1024 * 1024

TOKEN_TILE = 512
Q_TILE = 256
FFN_CHUNK = 1408
MOE_ROW_TILE = 1024
MOE_F_CHUNK = 512
NA_Q_ROWS = 4
NA_K_ROWS = 12


def _cparams(*sem):
    return pltpu.CompilerParams(dimension_semantics=sem, vmem_limit_bytes=VMEM_LIMIT_BYTES)


def _rms(x, g):
    return x * lax.rsqrt(jnp.mean(x * x, axis=-1, keepdims=True) + NORM_EPS) * g


def _dot(a, b):
    return jnp.dot(a, b, preferred_element_type=F32)


def _dot_nt(a, b):
    return lax.dot_general(a, b, (((1,), (1,)), ((), ())), preferred_element_type=F32)


def _resident(shape):
    zeros = (0,) * len(shape)
    return pl.BlockSpec(shape, lambda *_: zeros, pipeline_mode=pl.Buffered(1))


def _front_even_kernel(x_ref, g_ref, win_ref, qn_ref, kvn_ref, wqa_ref, wqr_ref, wk_ref, wv_ref, place_ref,
                       cq_ref, sq_ref, rk_ref,
                       qa_ref, ka_ref, va_ref, qb_ref, kb_ref, vb_ref):
    h = _rms(x_ref[...], g_ref[...]).astype(BF16)
    proj = _dot(h, win_ref[...])
    qa_ref[...] = proj[:, 0:512].astype(BF16)
    ka_ref[...] = proj[:, 512:1024].astype(BF16)
    va_ref[...] = proj[:, 1024:1536].astype(BF16)

    qn = _rms(proj[:, 1536:1792], qn_ref[...]).astype(BF16)
    q = _dot(qn, wqa_ref[...]) * cq_ref[...] + _dot(qn, wqr_ref[...]) * sq_ref[...]
    qb_ref[...] = q.astype(BF16)

    kvn = _rms(proj[:, 1792:1920], kvn_ref[...]).astype(BF16)
    t = proj[:, 1920:2048] * rk_ref[...]
    k_pe = t + pltpu.roll(t, 96, 1)
    k = _dot(kvn, wk_ref[...]) + _dot(k_pe.astype(BF16), place_ref[...])
    kb_ref[...] = k.astype(BF16)
    vb_ref[...] = _dot(kvn, wv_ref[...]).astype(BF16)


def _front_even(x, g, win, qnorm, kvnorm, wqa, wqr, wk, wv, place, cq, sq, rk, batch):
    n = x.shape[0]
    tm = TOKEN_TILE
    pt = SEQ // tm
    row = lambda p, b: (b * pt + p, 0)
    pos = lambda p, b: (p, 0)
    tok = lambda w: pl.BlockSpec((tm, w), row)
    out = lambda w: jax.ShapeDtypeStruct((n, w), BF16)
    return pl.pallas_call(
        _front_even_kernel,
        grid=(pt, batch),
        in_specs=[tok(D_MODEL), _resident((1, D_MODEL)), _resident(win.shape), _resident((1, B_Q_LORA)),
                  _resident((1, B_KV_LORA)), _resident(wqa.shape), _resident(wqr.shape), _resident(wk.shape),
                  _resident(wv.shape), _resident(place.shape),
                  pl.BlockSpec((tm, 1024), pos), pl.BlockSpec((tm, 1024), pos), pl.BlockSpec((tm, 128), pos)],
        out_specs=[tok(512), tok(512), tok(512), tok(1024), tok(1024), tok(512)],
        out_shape=[out(512), out(512), out(512), out(1024), out(1024), out(512)],
        compiler_params=_cparams("arbitrary", "arbitrary"),
        name="front_even",
    )(x, g, win, qnorm, kvnorm, wqa, wqr, wk, wv, place, cq, sq, rk)


A_TILES = SEQ // Q_TILE


def _head_lane_mask(shape):
    return lax.broadcasted_iota(jnp.int32, shape, 1) < HEAD_DIM


def _attn_a_kernel(q_ref, k_ref, v_ref, bias_ref, o_ref):
    def q_tile(i, carry):
        q0 = pl.multiple_of(i * Q_TILE, Q_TILE)
        q = q_ref[pl.ds(q0, Q_TILE), :]
        lo_mask = _head_lane_mask(q.shape)
        outs = []
        for head in range(2):
            qm = jnp.where(lo_mask if head == 0 else jnp.logical_not(lo_mask), q, jnp.zeros_like(q))
            s = [_dot_nt(qm, k_ref[j * Q_TILE:(j + 1) * Q_TILE, :]) + bias_ref[head, j - i + A_TILES - 1]
                 for j in range(A_TILES)]
            m = functools.reduce(jnp.maximum, [jnp.max(t, axis=-1, keepdims=True) for t in s])
            p = [jnp.exp(t - m) for t in s]
            l = functools.reduce(jnp.add, [jnp.sum(t, axis=-1, keepdims=True) for t in p])
            acc = functools.reduce(jnp.add, [_dot(p[j].astype(BF16), v_ref[j * Q_TILE:(j + 1) * Q_TILE, :])
                                             for j in range(A_TILES)])
            outs.append(acc / l)
        o_ref[pl.ds(q0, Q_TILE), :] = jnp.where(lo_mask, outs[0], outs[1]).astype(o_ref.dtype)
        return carry

    lax.fori_loop(0, A_TILES, q_tile, 0)


def _attn_a(qa, ka, va, bias, batch):
    n = qa.shape[0]
    blk = pl.BlockSpec((SEQ, 2 * HEAD_DIM), lambda hp, b: (b, hp))
    return pl.pallas_call(
        _attn_a_kernel,
        grid=(A_HEADS // 2, batch),
        in_specs=[blk, blk, blk,
                  pl.BlockSpec((2, 2 * A_TILES - 1, Q_TILE, Q_TILE), lambda hp, b: (hp, 0, 0, 0))],
        out_specs=blk,
        out_shape=jax.ShapeDtypeStruct((n, A_WIDTH), BF16),
        compiler_params=_cparams("arbitrary", "arbitrary"),
        name="attn_dilated",
    )(qa, ka, va, bias)


def _dilated_bias():
    o = np.arange(-(A_TILES - 1), A_TILES)[:, None, None]
    d = Q_TILE * o + np.arange(Q_TILE)[None, None, :] - np.arange(Q_TILE)[None, :, None]
    ad = np.abs(d)
    mult = np.zeros(d.shape, np.int32)
    for window, dilation in A_PATTERNS:
        mult += ((d % dilation == 0) & (ad <= window // 2)).astype(np.int32)
    logm = np.where(mult > 0, np.log(np.maximum(mult, 1).astype(np.float64)), 0.0).astype(np.float32)
    slopes = np.array([2.0 ** (-8.0 * (h + 1) / A_HEADS) for h in range(A_HEADS)], np.float32)
    bias = -jnp.asarray(slopes)[:, None, None, None] * jnp.asarray(ad.astype(np.float32))[None] + jnp.asarray(logm)[None]
    return jnp.where(jnp.asarray(mult > 0)[None], bias, NEG)


def _attn_mla_kernel(q_ref, k_ref, v_ref, o_ref):
    def q_tile(i, carry):
        q0 = pl.multiple_of(i * Q_TILE, Q_TILE)
        outs = []
        for head in range(2):
            lanes = slice(head * B_HEAD_LANES, (head + 1) * B_HEAD_LANES)
            s = _dot_nt(q_ref[pl.ds(q0, Q_TILE), lanes], k_ref[:, lanes])
            p = jnp.exp(s - jnp.max(s, axis=-1, keepdims=True))
            l = jnp.sum(p, axis=-1, keepdims=True)
            outs.append(_dot(p.astype(BF16), v_ref[...]) / l)
        lo_mask = _head_lane_mask(outs[0].shape)
        o_ref[pl.ds(q0, Q_TILE), :] = jnp.where(lo_mask, outs[0], outs[1]).astype(o_ref.dtype)
        return carry

    lax.fori_loop(0, SEQ // Q_TILE, q_tile, 0)


def _attn_mla(qb, kb, vb, batch):
    n = qb.shape[0]
    qk = pl.BlockSpec((SEQ, 2 * B_HEAD_LANES), lambda hp, b: (b, hp))
    vo = pl.BlockSpec((SEQ, 2 * B_V), lambda hp, b: (b, hp))
    return pl.pallas_call(
        _attn_mla_kernel,
        grid=(B_HEADS // 2, batch),
        in_specs=[qk, qk, vo],
        out_specs=vo,
        out_shape=jax.ShapeDtypeStruct((n, B_HEADS * B_V), BF16),
        compiler_params=_cparams("arbitrary", "arbitrary"),
        name="attn_latent",
    )(qb, kb, vb)


def _post_even_kernel(x_ref, oa_ref, ob_ref, woa_ref, wob_ref, g_ref, wg_ref, wu_ref, wd_ref, o_ref):
    x1 = x_ref[...] + _dot(oa_ref[...], woa_ref[...]) + _dot(ob_ref[...], wob_ref[...])
    h = _rms(x1, g_ref[...]).astype(BF16)
    y = None
    for c in range(D_FF // FFN_CHUNK):
        cols = slice(c * FFN_CHUNK, (c + 1) * FFN_CHUNK)
        a = jax.nn.silu(_dot(h, wg_ref[:, cols])) * _dot(h, wu_ref[:, cols])
        d = _dot(a.astype(BF16), wd_ref[cols, :])
        y = d if y is None else y + d
    o_ref[...] = x1 + y


def _post_even(x, oa, ob, woa, wob, g, wg, wu, wd):
    n = x.shape[0]
    tm = TOKEN_TILE
    tok = lambda w: pl.BlockSpec((tm, w), lambda i: (i, 0))
    return pl.pallas_call(
        _post_even_kernel,
        grid=(n // tm,),
        in_specs=[tok(D_MODEL), tok(512), tok(512), _resident(woa.shape), _resident(wob.shape),
                  _resident((1, D_MODEL)), _resident(wg.shape), _resident(wu.shape), _resident(wd.shape)],
        out_specs=tok(D_MODEL),
        out_shape=jax.ShapeDtypeStruct((n, D_MODEL), F32),
        compiler_params=_cparams("arbitrary"),
        name="post_even",
    )(x, oa, ob, woa, wob, g, wg, wu, wd)


def _front_odd_kernel(x_ref, g_ref, w_ref, q_ref, k_ref, v_ref):
    h = _rms(x_ref[...], g_ref[...]).astype(BF16)
    qkv = _dot(h, w_ref[...])
    q_ref[...] = qkv[:, 0:1024].astype(BF16)
    k_ref[...] = qkv[:, 1024:2048].astype(BF16)
    v_ref[...] = qkv[:, 2048:3072].astype(BF16)


def _front_odd(x, g, w):
    n = x.shape[0]
    tm = TOKEN_TILE
    tok = pl.BlockSpec((tm, D_MODEL), lambda i: (i, 0))
    out = jax.ShapeDtypeStruct((n, D_MODEL), BF16)
    return pl.pallas_call(
        _front_odd_kernel,
        grid=(n // tm,),
        in_specs=[tok, _resident((1, D_MODEL)), _resident(w.shape)],
        out_specs=[tok, tok, tok],
        out_shape=[out, out, out],
        compiler_params=_cparams("arbitrary"),
        name="front_odd",
    )(x, g, w)


NA_Q = NA_Q_ROWS * GRID_W
NA_K = NA_K_ROWS * GRID_W
NA_BLOCKS = GRID_ROWS // NA_Q_ROWS


def _na_key_row_start(block):
    return np.clip(NA_Q_ROWS * block - NA_ROWS // 2, 0, GRID_ROWS - NA_K_ROWS)


def _attn_na_kernel(q_ref, k_ref, v_ref, bias_ref, o_ref):
    def q_block(bi, carry):
        q0 = pl.multiple_of(bi * NA_Q, NA_Q)
        ks = jnp.clip(NA_Q_ROWS * bi - NA_ROWS // 2, 0, GRID_ROWS - NA_K_ROWS)
        k0 = pl.multiple_of(ks * GRID_W, GRID_W)
        case = jnp.where(bi == 0, 0, jnp.where(bi == NA_BLOCKS - 1, 2, 1))
        q = q_ref[pl.ds(q0, NA_Q), :]
        k = k_ref[pl.ds(k0, NA_K), :]
        v = v_ref[pl.ds(k0, NA_K), :]
        lo_mask = _head_lane_mask(q.shape)
        outs = []
        for head in range(2):
            qm = jnp.where(lo_mask if head == 0 else jnp.logical_not(lo_mask), q, jnp.zeros_like(q))
            s = _dot_nt(qm, k) + bias_ref[head, case]
            p = jnp.exp(s - jnp.max(s, axis=-1, keepdims=True))
            l = jnp.sum(p, axis=-1, keepdims=True)
            outs.append(_dot(p.astype(BF16), v) / l)
        o_ref[pl.ds(q0, NA_Q), :] = jnp.where(lo_mask, outs[0], outs[1]).astype(o_ref.dtype)
        return carry

    lax.fori_loop(0, NA_BLOCKS, q_block, 0)


def _attn_na(q, k, v, bias, batch):
    n = q.shape[0]
    blk = pl.BlockSpec((SEQ, 2 * HEAD_DIM), lambda hp, b: (b, hp))
    return pl.pallas_call(
        _attn_na_kernel,
        grid=(C_HEADS // 2, batch),
        in_specs=[blk, blk, blk, pl.BlockSpec((2, 3, NA_Q, NA_K), lambda hp, b: (hp, 0, 0, 0))],
        out_specs=blk,
        out_shape=jax.ShapeDtypeStruct((n, C_HEADS * HEAD_DIM), BF16),
        compiler_params=_cparams("arbitrary", "arbitrary"),
        name="attn_neighbourhood",
    )(q, k, v, bias)


def _na_bias(rpb):
    row_sel, row_ok = [], []
    for block in (0, 1, NA_BLOCKS - 1):
        qi = NA_Q_ROWS * block + np.arange(NA_Q_ROWS)[:, None]
        kr = _na_key_row_start(block) + np.arange(NA_K_ROWS)[None, :]
        rs = np.clip(qi - NA_ROWS // 2, 0, GRID_ROWS - NA_ROWS)
        row_ok.append((kr >= rs) & (kr < rs + NA_ROWS))
        row_sel.append(np.eye(2 * NA_ROWS - 1, dtype=np.float32)[np.clip(kr - qi + NA_ROWS - 1, 0, 2 * NA_ROWS - 2)])
    qc = np.arange(GRID_W)[:, None]
    kc = np.arange(GRID_W)[None, :]
    cs = np.clip(qc - NA_COLS // 2, 0, GRID_W - NA_COLS)
    col_ok = (kc >= cs) & (kc < cs + NA_COLS)
    col_sel = np.eye(2 * NA_COLS - 1, dtype=np.float32)[np.clip(kc - qc + NA_COLS - 1, 0, 2 * NA_COLS - 2)]
    row_sel, row_ok = np.stack(row_sel), np.stack(row_ok)
    table = jnp.einsum("cqka,hab,xyb->hcqxky", jnp.asarray(row_sel), rpb.astype(F32), jnp.asarray(col_sel),
                       precision=lax.Precision.HIGHEST)
    valid = row_ok[:, :, None, :, None] & col_ok[None, None, :, None, :]
    return jnp.where(jnp.asarray(valid)[None], table, NEG).reshape(rpb.shape[0], 3, NA_Q, NA_K)


def _post_odd_kernel(x_ref, att_ref, wo_ref, g_ref, wr_ref, x1_ref, xn_ref, idx_ref, gate_ref):
    x1 = x_ref[...] + _dot(att_ref[...], wo_ref[...])
    x1_ref[...] = x1
    xn = _rms(x1, g_ref[...])
    xn_ref[...] = xn
    logits = jnp.dot(xn, wr_ref[...], precision=lax.Precision.HIGHEST, preferred_element_type=F32)
    lane = lax.broadcasted_iota(jnp.int32, logits.shape, 1)
    logits = jnp.where(lane < N_EXPERTS, logits, -jnp.inf)
    m1 = jnp.max(logits, axis=-1, keepdims=True)
    i1 = jnp.min(jnp.where(logits == m1, lane, 128), axis=-1, keepdims=True)
    rest = jnp.where(lane == i1, -jnp.inf, logits)
    m2 = jnp.max(rest, axis=-1, keepdims=True)
    i2 = jnp.min(jnp.where(rest == m2, lane, 128), axis=-1, keepdims=True)
    e2 = jnp.exp(m2 - m1)
    w1 = 1.0 / (1.0 + e2)
    w2 = e2 / (1.0 + e2)
    idx_ref[...] = jnp.where(lane == 0, i1, jnp.where(lane == 1, i2, 0))
    gate_ref[...] = jnp.where(lane == 0, w1, jnp.where(lane == 1, w2, 0.0))


def _post_odd(x, att, wo, g, wr):
    n = x.shape[0]
    tm = TOKEN_TILE
    tok = lambda w: pl.BlockSpec((tm, w), lambda i: (i, 0))
    return pl.pallas_call(
        _post_odd_kernel,
        grid=(n // tm,),
        in_specs=[tok(D_MODEL), tok(D_MODEL), _resident(wo.shape), _resident((1, D_MODEL)), _resident(wr.shape)],
        out_specs=[tok(D_MODEL), tok(D_MODEL), tok(128), tok(128)],
        out_shape=[jax.ShapeDtypeStruct((n, D_MODEL), F32), jax.ShapeDtypeStruct((n, D_MODEL), F32),
                   jax.ShapeDtypeStruct((n, 128), jnp.int32), jax.ShapeDtypeStruct((n, 128), F32)],
        compiler_params=_cparams("arbitrary"),
        name="post_odd_router",
    )(x, att, wo, g, wr)


def _moe_kernel(te_ref, nv_ref, nt_ref, src_ref, dst_ref, xn_hbm, wg_ref, wu_ref, wd_ref, y_hbm,
                xbuf, acc, sem_in, sem_out):
    del te_ref
    j = pl.program_id(0)
    f = pl.program_id(1)
    ts = MOE_ROW_TILE
    active = j < nt_ref[0]

    def row_in(r, tok):
        return pltpu.make_async_copy(xn_hbm.at[pl.ds(tok, 1), :], xbuf.at[pl.ds(r, 1), :], sem_in)

    def row_out(r, dst):
        return pltpu.make_async_copy(acc.at[pl.ds(r, 1), :], y_hbm.at[pl.ds(dst, 1), :], sem_out)

    @pl.when(jnp.logical_and(active, f == 0))
    def _gather():
        def start(r, c):
            row_in(r, src_ref[0, 0, r]).start()
            return c
        lax.fori_loop(0, ts, start, 0)

        def wait(r, c):
            row_in(r, 0).wait()
            return c
        lax.fori_loop(0, ts, wait, 0)
        acc[...] = jnp.zeros_like(acc)

    @pl.when(active)
    def _compute():
        xb = xbuf[...].astype(BF16)
        a = jax.nn.silu(_dot(xb, wg_ref[0])) * _dot(xb, wu_ref[0])
        acc[...] += _dot(a.astype(BF16), wd_ref[0])

    @pl.when(jnp.logical_and(active, f == pl.num_programs(1) - 1))
    def _scatter():
        nv = nv_ref[j]

        def start(r, c):
            row_out(r, dst_ref[0, 0, r]).start()
            return c
        lax.fori_loop(0, nv, start, 0)

        def wait(r, c):
            row_out(r, 0).wait()
            return c
        lax.fori_loop(0, nv, wait, 0)


def _moe(xn, wg, wu, wd, tile_expert, tile_rows, n_tiles, src, dst):
    n = xn.shape[0]
    ts = MOE_ROW_TILE
    fc = MOE_F_CHUNK
    tiles = src.shape[0]
    nf = D_FF_EXPERT // fc

    def f_eff(j, f, nt):
        return jnp.where(j < nt[0], f, nf - 1)

    smem = lambda: pl.BlockSpec((1, 1, ts), lambda j, f, te, nv, nt: (j, 0, 0), memory_space=pltpu.SMEM)
    grid_spec = pltpu.PrefetchScalarGridSpec(
        num_scalar_prefetch=3,
        grid=(tiles, nf),
        in_specs=[smem(), smem(), pl.BlockSpec(memory_space=pl.ANY),
                  pl.BlockSpec((1, D_MODEL, fc), lambda j, f, te, nv, nt: (te[j], 0, f_eff(j, f, nt))),
                  pl.BlockSpec((1, D_MODEL, fc), lambda j, f, te, nv, nt: (te[j], 0, f_eff(j, f, nt))),
                  pl.BlockSpec((1, fc, D_MODEL), lambda j, f, te, nv, nt: (te[j], f_eff(j, f, nt), 0))],
        out_specs=pl.BlockSpec(memory_space=pl.ANY),
        scratch_shapes=[pltpu.VMEM((ts, D_MODEL), F32), pltpu.VMEM((ts, D_MODEL), F32),
                        pltpu.SemaphoreType.DMA, pltpu.SemaphoreType.DMA],
    )
    return pl.pallas_call(
        _moe_kernel,
        grid_spec=grid_spec,
        out_shape=jax.ShapeDtypeStruct((2 * n, D_MODEL), F32),
        compiler_params=_cparams("arbitrary", "arbitrary"),
        name="moe_experts",
    )(tile_expert, tile_rows, n_tiles, src, dst, xn, wg, wu, wd)


def _moe_plan(expert_idx):
    n = expert_idx.shape[0]
    ts = MOE_ROW_TILE
    tiles = 2 * n // ts + N_EXPERTS
    flat = expert_idx.reshape(-1)
    onehot = (flat[:, None] == jnp.arange(N_EXPERTS, dtype=jnp.int32)[None]).astype(jnp.int32)
    before = jnp.cumsum(onehot, axis=0) - onehot
    rank = jnp.sum(before * onehot, axis=1)
    counts = jnp.sum(onehot, axis=0)
    padded = (counts + ts - 1) // ts * ts
    ends = jnp.cumsum(padded)
    starts = ends - padded
    pos = starts[flat] + rank
    a = jnp.arange(2 * n, dtype=jnp.int32)
    token, k = a // 2, a % 2
    src = jnp.zeros((tiles * ts,), jnp.int32).at[pos].set(token)
    dst = jnp.zeros((tiles * ts,), jnp.int32).at[pos].set(k * n + token)
    n_tiles = ends[-1] // ts
    tile_start = jnp.arange(tiles, dtype=jnp.int32) * ts
    tile_expert = jnp.minimum(jnp.searchsorted(ends, tile_start, side="right"), N_EXPERTS - 1).astype(jnp.int32)
    tile_rows = jnp.clip(counts[tile_expert] - (tile_start - starts[tile_expert]), 0, ts).astype(jnp.int32)
    last = tile_expert[jnp.maximum(n_tiles - 1, 0)]
    tile_expert = jnp.where(tile_start < ends[-1], tile_expert, last)
    return (tile_expert, tile_rows, n_tiles.reshape(1).astype(jnp.int32),
            src.reshape(tiles, 1, ts), dst.reshape(tiles, 1, ts))


def _combine_kernel(x_ref, y0_ref, y1_ref, gate_ref, g_ref, o_ref, *, final_norm):
    gates = gate_ref[...]
    y = gates[:, 0:1] * y0_ref[...] + gates[:, 1:2] * y1_ref[...]
    x = x_ref[...] + y
    o_ref[...] = _rms(x, g_ref[...]) if final_norm else x


def _combine(x1, y, gates, g, final_norm):
    n = x1.shape[0]
    tm = TOKEN_TILE
    nb = n // tm
    tok = lambda w: pl.BlockSpec((tm, w), lambda i: (i, 0))
    return pl.pallas_call(
        functools.partial(_combine_kernel, final_norm=final_norm),
        grid=(nb,),
        in_specs=[tok(D_MODEL), tok(D_MODEL), pl.BlockSpec((tm, D_MODEL), lambda i: (nb + i, 0)), tok(128),
                  _resident((1, D_MODEL))],
        out_specs=tok(D_MODEL),
        out_shape=jax.ShapeDtypeStruct((n, D_MODEL), F32),
        compiler_params=_cparams("arbitrary"),
        name="moe_combine",
    )(x1, y, y, gates, g)


def _rope_tables():
    half = B_ROPE // 2
    freqs = ROPE_THETA ** (-jnp.arange(half, dtype=F32) / half)
    ang = jnp.arange(SEQ, dtype=F32)[:, None] * freqs[None]
    cos = jnp.concatenate([jnp.cos(ang), jnp.cos(ang)], axis=-1)
    sin = jnp.concatenate([jnp.sin(ang), jnp.sin(ang)], axis=-1)
    scale = (B_NOPE + B_ROPE) ** -0.5
    ones = jnp.ones((SEQ, B_NOPE), F32)
    zeros_n = jnp.zeros((SEQ, B_NOPE), F32)
    zeros_p = jnp.zeros((SEQ, B_HEAD_LANES - B_NOPE - B_ROPE), F32)
    cq = jnp.tile(jnp.concatenate([ones, cos, zeros_p], axis=-1), (1, B_HEADS)) * scale
    sq = jnp.tile(jnp.concatenate([zeros_n, sin, zeros_p], axis=-1), (1, B_HEADS)) * scale
    rk = jnp.concatenate([cos, sin, jnp.zeros((SEQ, 64), F32)], axis=-1)
    return cq, sq, rk


def _half_swap(w):
    half = B_ROPE // 2
    return jnp.concatenate([-w[..., half:], w[..., :half]], axis=-1)


def _prep_even(w_in, w_uq, w_ukv, w_out):
    scale_a = HEAD_DIM ** -0.5
    k_rope = w_in[:, 3 * A_WIDTH + B_Q_LORA + B_KV_LORA:]
    win = jnp.concatenate([w_in[:, :A_WIDTH] * scale_a, w_in[:, A_WIDTH:3 * A_WIDTH + B_Q_LORA + B_KV_LORA],
                           k_rope, _half_swap(k_rope), jnp.zeros((D_MODEL, 64), F32)], axis=-1).astype(BF16)
    uq = w_uq.reshape(B_Q_LORA, B_HEADS, B_NOPE + B_ROPE)
    pad = jnp.zeros((B_Q_LORA, B_HEADS, B_HEAD_LANES - B_NOPE - B_ROPE), F32)
    wqa = jnp.concatenate([uq, pad], axis=-1).reshape(B_Q_LORA, -1).astype(BF16)
    wqr = jnp.concatenate([jnp.zeros_like(uq[..., :B_NOPE]), _half_swap(uq[..., B_NOPE:]), pad], axis=-1)
    wqr = wqr.reshape(B_Q_LORA, -1).astype(BF16)
    ukv = w_ukv.reshape(B_KV_LORA, B_HEADS, B_NOPE + B_V)
    wk = jnp.concatenate([ukv[..., :B_NOPE], jnp.zeros((B_KV_LORA, B_HEADS, B_HEAD_LANES - B_NOPE), F32)], axis=-1)
    wk = wk.reshape(B_KV_LORA, -1).astype(BF16)
    wv = ukv[..., B_NOPE:].reshape(B_KV_LORA, -1).astype(BF16)
    place = np.zeros((128, B_HEADS, B_HEAD_LANES), np.float32)
    for r in range(B_ROPE):
        place[r, :, B_NOPE + r] = 1.0
    place = jnp.asarray(place.reshape(128, -1), BF16)
    wo = w_out.astype(BF16)
    return win, wqa, wqr, wk, wv, place, wo[:A_WIDTH], wo[A_WIDTH:]


def kernel(x, ln_mix_e, w_in_e, mla_q_norm, mla_w_uq, mla_kv_norm, mla_w_ukv, w_out_e, ln_ffn_e, ffn_w_gate,
           ffn_w_up, ffn_w_down, ln_mix_o, na_w_qkv, na_rpb, na_w_out, ln_ffn_o, moe_router, moe_w_gate,
           moe_w_up, moe_w_down, ln_final):
    batch, seq, d = x.shape
    assert (seq, d) == (SEQ, D_MODEL)
    n = batch * seq
    depth = ln_mix_e.shape[0] + ln_mix_o.shape[0]
    row = lambda v: v.reshape(1, -1).astype(F32)
    cq, sq, rk = _rope_tables()
    bias_a = _dilated_bias()
    xf = x.reshape(n, d)
    for layer in range(depth):
        i = layer // 2
        if layer % 2 == 0:
            win, wqa, wqr, wk, wv, place, woa, wob = _prep_even(w_in_e[i], mla_w_uq[i], mla_w_ukv[i], w_out_e[i])
            qa, ka, va, qb, kb, vb = _front_even(xf, row(ln_mix_e[i]), win, row(mla_q_norm[i]), row(mla_kv_norm[i]),
                                                 wqa, wqr, wk, wv, place, cq, sq, rk, batch)
            oa = _attn_a(qa, ka, va, bias_a, batch)
            ob = _attn_mla(qb, kb, vb, batch)
            xf = _post_even(xf, oa, ob, woa, wob, row(ln_ffn_e[i]), ffn_w_gate[i].astype(BF16),
                            ffn_w_up[i].astype(BF16), ffn_w_down[i].astype(BF16))
        else:
            wqkv = jnp.concatenate([na_w_qkv[i][:, :D_MODEL] * HEAD_DIM ** -0.5, na_w_qkv[i][:, D_MODEL:]], axis=-1)
            q, k, v = _front_odd(xf, row(ln_mix_o[i]), wqkv.astype(BF16))
            att = _attn_na(q, k, v, _na_bias(na_rpb[i]), batch)
            wr = jnp.concatenate([moe_router[i].astype(F32), jnp.zeros((D_MODEL, 128 - N_EXPERTS), F32)], axis=-1)
            x1, xn, idx, gates = _post_odd(xf, att, na_w_out[i].astype(BF16), row(ln_ffn_o[i]), wr)
            plan = _moe_plan(idx[:, :2])
            y = _moe(xn, moe_w_gate[i].astype(BF16), moe_w_up[i].astype(BF16), moe_w_down[i].astype(BF16), *plan)
            last = layer == depth - 1
            xf = _combine(x1, y, gates, row(ln_final), final_norm=last)
    if depth % 2 == 1:
        raise NotImplementedError("final norm is fused into the last odd layer")
    return xf.reshape(batch, seq, d)
```

```python
import functools
import math

import numpy as np
import jax
import jax.numpy as jnp
from jax import lax
from jax.experimental import pallas as pl
from jax.experimental.pallas import tpu as pltpu

F32 = jnp.float32
BF16 = jnp.bfloat16

D_MODEL = 1024
SEQ = 2048
HEAD_DIM = 64
NORM_EPS = 1e-6
NEG = -1e30

A_HEADS = 8
A_WIDTH = A_HEADS * HEAD_DIM
A_PATTERNS = ((128, 1), (512, 4), (2048, 16))

B_HEADS = 8
B_Q_LORA = 256
B_KV_LORA = 128
B_NOPE = 64
B_ROPE = 32
B_V = 64
ROPE_THETA = 10000.0
B_HEAD_LANES = 128

C_HEADS = 16
GRID_W = 64
GRID_ROWS = SEQ // GRID_W
NA_ROWS = 8
NA_COLS = 16

D_FF = 2816
N_EXPERTS = 8
D_FF_EXPERT = 3584

VMEM_LIMIT_BYTES = 56 * 1024 * 1024

TOKEN_TILE = 512
Q_TILE = 256
FFN_CHUNK = 1408
MOE_ROW_TILE = 1024
MOE_F_CHUNK = 512
MOE_TOKEN_TILE = 512
NA_Q_ROWS = 4
NA_K_ROWS = 12


def _cparams(*sem):
    return pltpu.CompilerParams(dimension_semantics=sem, vmem_limit_bytes=VMEM_LIMIT_BYTES)


def _rms(x, g):
    return x * lax.rsqrt(jnp.mean(x * x, axis=-1, keepdims=True) + NORM_EPS) * g


def _dot(a, b):
    return jnp.dot(a, b, preferred_element_type=F32)


def _dot_nt(a, b):
    return lax.dot_general(a, b, (((1,), (1,)), ((), ())), preferred_element_type=F32)


def _resident(shape):
    zeros = (0,) * len(shape)
    return pl.BlockSpec(shape, lambda *_: zeros, pipeline_mode=pl.Buffered(1))


def _front_even_kernel(x_ref, g_ref, win_ref, qn_ref, kvn_ref, wqa_ref, wqr_ref, wk_ref, wv_ref, place_ref,
                       cq_ref, sq_ref, rk_ref,
                       qa_ref, ka_ref, va_ref, qb_ref, kb_ref, vb_ref):
    h = _rms(x_ref[...], g_ref[...]).astype(BF16)
    proj = _dot(h, win_ref[...])
    qa_ref[...] = proj[:, 0:512].astype(BF16)
    ka_ref[...] = proj[:, 512:1024].astype(BF16)
    va_ref[...] = proj[:, 1024:1536].astype(BF16)

    qn = _rms(proj[:, 1536:1792], qn_ref[...]).astype(BF16)
    q = _dot(qn, wqa_ref[...]) * cq_ref[...] + _dot(qn, wqr_ref[...]) * sq_ref[...]
    qb_ref[...] = q.astype(BF16)

    kvn = _rms(proj[:, 1792:1920], kvn_ref[...]).astype(BF16)
    t = proj[:, 1920:2048] * rk_ref[...]
    k_pe = t + pltpu.roll(t, 96, 1)
    k = _dot(kvn, wk_ref[...]) + _dot(k_pe.astype(BF16), place_ref[...])
    kb_ref[...] = k.astype(BF16)
    vb_ref[...] = _dot(kvn, wv_ref[...]).astype(BF16)


def _front_even(x, g, win, qnorm, kvnorm, wqa, wqr, wk, wv, place, cq, sq, rk, batch):
    n = x.shape[0]
    tm = TOKEN_TILE
    pt = SEQ // tm
    row = lambda p, b: (b * pt + p, 0)
    pos = lambda p, b: (p, 0)
    tok = lambda w: pl.BlockSpec((tm, w), row)
    out = lambda w: jax.ShapeDtypeStruct((n, w), BF16)
    return pl.pallas_call(
        _front_even_kernel,
        grid=(pt, batch),
        in_specs=[tok(D_MODEL), _resident((1, D_MODEL)), _resident(win.shape), _resident((1, B_Q_LORA)),
                  _resident((1, B_KV_LORA)), _resident(wqa.shape), _resident(wqr.shape), _resident(wk.shape),
                  _resident(wv.shape), _resident(place.shape),
                  pl.BlockSpec((tm, 1024), pos), pl.BlockSpec((tm, 1024), pos), pl.BlockSpec((tm, 128), pos)],
        out_specs=[tok(512), tok(512), tok(512), tok(1024), tok(1024), tok(512)],
        out_shape=[out(512), out(512), out(512), out(1024), out(1024), out(512)],
        compiler_params=_cparams("arbitrary", "arbitrary"),
        name="front_even",
    )(x, g, win, qnorm, kvnorm, wqa, wqr, wk, wv, place, cq, sq, rk)


A_TILES = SEQ // Q_TILE


def _head_lane_mask(shape):
    return lax.broadcasted_iota(jnp.int32, shape, 1) < HEAD_DIM


def _attn_a_kernel(q_ref, k_ref, v_ref, bias_ref, o_ref):
    def q_tile(i, carry):
        q0 = pl.multiple_of(i * Q_TILE, Q_TILE)
        q = q_ref[pl.ds(q0, Q_TILE), :]
        lo_mask = _head_lane_mask(q.shape)
        outs = []
        for head in range(2):
            qm = jnp.where(lo_mask if head == 0 else jnp.logical_not(lo_mask), q, jnp.zeros_like(q))
            s = [_dot_nt(qm, k_ref[j * Q_TILE:(j + 1) * Q_TILE, :]) + bias_ref[head, j - i + A_TILES - 1]
                 for j in range(A_TILES)]
            m = functools.reduce(jnp.maximum, [jnp.max(t, axis=-1, keepdims=True) for t in s])
            p = [jnp.exp(t - m) for t in s]
            l = functools.reduce(jnp.add, [jnp.sum(t, axis=-1, keepdims=True) for t in p])
            acc = functools.reduce(jnp.add, [_dot(p[j].astype(BF16), v_ref[j * Q_TILE:(j + 1) * Q_TILE, :])
                                             for j in range(A_TILES)])
            outs.append(acc / l)
        o_ref[pl.ds(q0, Q_TILE), :] = jnp.where(lo_mask, outs[0], outs[1]).astype(o_ref.dtype)
        return carry

    lax.fori_loop(0, A_TILES, q_tile, 0)


def _attn_a(qa, ka, va, bias, batch):
    n = qa.shape[0]
    blk = pl.BlockSpec((SEQ, 2 * HEAD_DIM), lambda hp, b: (b, hp))
    return pl.pallas_call(
        _attn_a_kernel,
        grid=(A_HEADS // 2, batch),
        in_specs=[blk, blk, blk,
                  pl.BlockSpec((2, 2 * A_TILES - 1, Q_TILE, Q_TILE), lambda hp, b: (hp, 0, 0, 0))],
        out_specs=blk,
        out_shape=jax.ShapeDtypeStruct((n, A_WIDTH), BF16),
        compiler_params=_cparams("arbitrary", "arbitrary"),
        name="attn_dilated",
    )(qa, ka, va, bias)


def _dilated_bias():
    o = np.arange(-(A_TILES - 1), A_TILES)[:, None, None]
    d = Q_TILE * o + np.arange(Q_TILE)[None, None, :] - np.arange(Q_TILE)[None, :, None]
    ad = np.abs(d)
    mult = np.zeros(d.shape, np.int32)
    for window, dilation in A_PATTERNS:
        mult += ((d % dilation == 0) & (ad <= window // 2)).astype(np.int32)
    logm = np.where(mult > 0, np.log(np.maximum(mult, 1).astype(np.float64)), 0.0).astype(np.float32)
    slopes = np.array([2.0 ** (-8.0 * (h + 1) / A_HEADS) for h in range(A_HEADS)], np.float32)
    bias = -jnp.asarray(slopes)[:, None, None, None] * jnp.asarray(ad.astype(np.float32))[None] + jnp.asarray(logm)[None]
    return jnp.where(jnp.asarray(mult > 0)[None], bias, NEG)


def _attn_mla_kernel(q_ref, k_ref, v_ref, o_ref):
    def q_tile(i, carry):
        q0 = pl.multiple_of(i * Q_TILE, Q_TILE)
        outs = []
        for head in range(2):
            lanes = slice(head * B_HEAD_LANES, (head + 1) * B_HEAD_LANES)
            s = _dot_nt(q_ref[pl.ds(q0, Q_TILE), lanes], k_ref[:, lanes])
            p = jnp.exp(s - jnp.max(s, axis=-1, keepdims=True))
            l = jnp.sum(p, axis=-1, keepdims=True)
            outs.append(_dot(p.astype(BF16), v_ref[...]) / l)
        lo_mask = _head_lane_mask(outs[0].shape)
        o_ref[pl.ds(q0, Q_TILE), :] = jnp.where(lo_mask, outs[0], outs[1]).astype(o_ref.dtype)
        return carry

    lax.fori_loop(0, SEQ // Q_TILE, q_tile, 0)


def _attn_mla(qb, kb, vb, batch):
    n = qb.shape[0]
    qk = pl.BlockSpec((SEQ, 2 * B_HEAD_LANES), lambda hp, b: (b, hp))
    vo = pl.BlockSpec((SEQ, 2 * B_V), lambda hp, b: (b, hp))
    return pl.pallas_call(
        _attn_mla_kernel,
        grid=(B_HEADS // 2, batch),
        in_specs=[qk, qk, vo],
        out_specs=vo,
        out_shape=jax.ShapeDtypeStruct((n, B_HEADS * B_V), BF16),
        compiler_params=_cparams("arbitrary", "arbitrary"),
        name="attn_latent",
    )(qb, kb, vb)


def _post_even_kernel(x_ref, oa_ref, ob_ref, woa_ref, wob_ref, g_ref, wg_ref, wu_ref, wd_ref, o_ref):
    x1 = x_ref[...] + _dot(oa_ref[...], woa_ref[...]) + _dot(ob_ref[...], wob_ref[...])
    h = _rms(x1, g_ref[...]).astype(BF16)
    y = None
    for c in range(D_FF // FFN_CHUNK):
        cols = slice(c * FFN_CHUNK, (c + 1) * FFN_CHUNK)
        a = jax.nn.silu(_dot(h, wg_ref[:, cols])) * _dot(h, wu_ref[:, cols])
        d = _dot(a.astype(BF16), wd_ref[cols, :])
        y = d if y is None else y + d
    o_ref[...] = x1 + y


def _post_even(x, oa, ob, woa, wob, g, wg, wu, wd):
    n = x.shape[0]
    tm = TOKEN_TILE
    tok = lambda w: pl.BlockSpec((tm, w), lambda i: (i, 0))
    return pl.pallas_call(
        _post_even_kernel,
        grid=(n // tm,),
        in_specs=[tok(D_MODEL), tok(512), tok(512), _resident(woa.shape), _resident(wob.shape),
                  _resident((1, D_MODEL)), _resident(wg.shape), _resident(wu.shape), _resident(wd.shape)],
        out_specs=tok(D_MODEL),
        out_shape=jax.ShapeDtypeStruct((n, D_MODEL), F32),
        compiler_params=_cparams("arbitrary"),
        name="post_even",
    )(x, oa, ob, woa, wob, g, wg, wu, wd)


def _front_odd_kernel(x_ref, g_ref, w_ref, q_ref, k_ref, v_ref):
    h = _rms(x_ref[...], g_ref[...]).astype(BF16)
    qkv = _dot(h, w_ref[...])
    q_ref[...] = qkv[:, 0:1024].astype(BF16)
    k_ref[...] = qkv[:, 1024:2048].astype(BF16)
    v_ref[...] = qkv[:, 2048:3072].astype(BF16)


def _front_odd(x, g, w):
    n = x.shape[0]
    tm = TOKEN_TILE
    tok = pl.BlockSpec((tm, D_MODEL), lambda i: (i, 0))
    out = jax.ShapeDtypeStruct((n, D_MODEL), BF16)
    return pl.pallas_call(
        _front_odd_kernel,
        grid=(n // tm,),
        in_specs=[tok, _resident((1, D_MODEL)), _resident(w.shape)],
        out_specs=[tok, tok, tok],
        out_shape=[out, out, out],
        compiler_params=_cparams("arbitrary"),
        name="front_odd",
    )(x, g, w)


NA_Q = NA_Q_ROWS * GRID_W
NA_K = NA_K_ROWS * GRID_W
NA_BLOCKS = GRID_ROWS // NA_Q_ROWS


def _na_key_row_start(block):
    return np.clip(NA_Q_ROWS * block - NA_ROWS // 2, 0, GRID_ROWS - NA_K_ROWS)


def _attn_na_kernel(q_ref, k_ref, v_ref, bias_ref, o_ref):
    def q_block(bi, carry):
        q0 = pl.multiple_of(bi * NA_Q, NA_Q)
        ks = jnp.clip(NA_Q_ROWS * bi - NA_ROWS // 2, 0, GRID_ROWS - NA_K_ROWS)
        k0 = pl.multiple_of(ks * GRID_W, GRID_W)
        case = jnp.where(bi == 0, 0, jnp.where(bi == NA_BLOCKS - 1, 2, 1))
        q = q_ref[pl.ds(q0, NA_Q), :]
        k = k_ref[pl.ds(k0, NA_K), :]
        v = v_ref[pl.ds(k0, NA_K), :]
        lo_mask = _head_lane_mask(q.shape)
        outs = []
        for head in range(2):
            qm = jnp.where(lo_mask if head == 0 else jnp.logical_not(lo_mask), q, jnp.zeros_like(q))
            s = _dot_nt(qm, k) + bias_ref[head, case]
            p = jnp.exp(s - jnp.max(s, axis=-1, keepdims=True))
            l = jnp.sum(p, axis=-1, keepdims=True)
            outs.append(_dot(p.astype(BF16), v) / l)
        o_ref[pl.ds(q0, NA_Q), :] = jnp.where(lo_mask, outs[0], outs[1]).astype(o_ref.dtype)
        return carry

    lax.fori_loop(0, NA_BLOCKS, q_block, 0)


def _attn_na(q, k, v, bias, batch):
    n = q.shape[0]
    blk = pl.BlockSpec((SEQ, 2 * HEAD_DIM), lambda hp, b: (b, hp))
    return pl.pallas_call(
        _attn_na_kernel,
        grid=(C_HEADS // 2, batch),
        in_specs=[blk, blk, blk, pl.BlockSpec((2, 3, NA_Q, NA_K), lambda hp, b: (hp, 0, 0, 0))],
        out_specs=blk,
        out_shape=jax.ShapeDtypeStruct((n, C_HEADS * HEAD_DIM), BF16),
        compiler_params=_cparams("arbitrary", "arbitrary"),
        name="attn_neighbourhood",
    )(q, k, v, bias)


def _na_bias(rpb):
    row_sel, row_ok = [], []
    for block in (0, 1, NA_BLOCKS - 1):
        qi = NA_Q_ROWS * block + np.arange(NA_Q_ROWS)[:, None]
        kr = _na_key_row_start(block) + np.arange(NA_K_ROWS)[None, :]
        rs = np.clip(qi - NA_ROWS // 2, 0, GRID_ROWS - NA_ROWS)
        row_ok.append((kr >= rs) & (kr < rs + NA_ROWS))
        row_sel.append(np.eye(2 * NA_ROWS - 1, dtype=np.float32)[np.clip(kr - qi + NA_ROWS - 1, 0, 2 * NA_ROWS - 2)])
    qc = np.arange(GRID_W)[:, None]
    kc = np.arange(GRID_W)[None, :]
    cs = np.clip(qc - NA_COLS // 2, 0, GRID_W - NA_COLS)
    col_ok = (kc >= cs) & (kc < cs + NA_COLS)
    col_sel = np.eye(2 * NA_COLS - 1, dtype=np.float32)[np.clip(kc - qc + NA_COLS - 1, 0, 2 * NA_COLS - 2)]
    row_sel, row_ok = np.stack(row_sel), np.stack(row_ok)
    table = jnp.einsum("cqka,hab,xyb->hcqxky", jnp.asarray(row_sel), rpb.astype(F32), jnp.asarray(col_sel),
                       precision=lax.Precision.HIGHEST)
    valid = row_ok[:, :, None, :, None] & col_ok[None, None, :, None, :]
    return jnp.where(jnp.asarray(valid)[None], table, NEG).reshape(rpb.shape[0], 3, NA_Q, NA_K)


def _post_odd_kernel(x_ref, att_ref, wo_ref, g_ref, wr_ref, x1_ref, xn_ref, idx_ref, gate_ref):
    x1 = x_ref[...] + _dot(att_ref[...], wo_ref[...])
    x1_ref[...] = x1
    xn = _rms(x1, g_ref[...])
    xn_ref[...] = xn
    logits = jnp.dot(xn, wr_ref[...], precision=lax.Precision.HIGHEST, preferred_element_type=F32)
    lane = lax.broadcasted_iota(jnp.int32, logits.shape, 1)
    logits = jnp.where(lane < N_EXPERTS, logits, -jnp.inf)
    m1 = jnp.max(logits, axis=-1, keepdims=True)
    i1 = jnp.min(jnp.where(logits == m1, lane, 128), axis=-1, keepdims=True)
    rest = jnp.where(lane == i1, -jnp.inf, logits)
    m2 = jnp.max(rest, axis=-1, keepdims=True)
    i2 = jnp.min(jnp.where(rest == m2, lane, 128), axis=-1, keepdims=True)
    e2 = jnp.exp(m2 - m1)
    w1 = 1.0 / (1.0 + e2)
    w2 = e2 / (1.0 + e2)
    idx_ref[...] = jnp.where(lane == 0, i1, jnp.where(lane == 1, i2, 0))
    gate_ref[...] = jnp.where(lane == 0, w1, jnp.where(lane == 1, w2, 0.0))


def _post_odd(x, att, wo, g, wr):
    n = x.shape[0]
    tm = TOKEN_TILE
    tok = lambda w: pl.BlockSpec((tm, w), lambda i: (i, 0))
    return pl.pallas_call(
        _post_odd_kernel,
        grid=(n // tm,),
        in_specs=[tok(D_MODEL), tok(D_MODEL), _resident(wo.shape), _resident((1, D_MODEL)), _resident(wr.shape)],
        out_specs=[tok(D_MODEL), tok(D_MODEL), tok(128), tok(128)],
        out_shape=[jax.ShapeDtypeStruct((n, D_MODEL), F32), jax.ShapeDtypeStruct((n, D_MODEL), F32),
                   jax.ShapeDtypeStruct((n, 128), jnp.int32), jax.ShapeDtypeStruct((n, 128), F32)],
        compiler_params=_cparams("arbitrary"),
        name="post_odd_router",
    )(x, att, wo, g, wr)


def _moe_plan(expert_idx):
    n = expert_idx.shape[0]
    ts = MOE_ROW_TILE
    tiles = 2 * n // ts + N_EXPERTS
    flat = expert_idx.reshape(-1)
    onehot = (flat[:, None] == jnp.arange(N_EXPERTS, dtype=jnp.int32)[None]).astype(jnp.int32)
    before = jnp.cumsum(onehot, axis=0) - onehot
    counts = jnp.sum(onehot, axis=0)
    padded = (counts + ts - 1) // ts * ts
    ends = jnp.cumsum(padded)
    starts = ends - padded
    pos = jnp.sum(onehot * (starts[None] + before), axis=1)
    total = ends[-1]
    tile_start = jnp.arange(tiles, dtype=jnp.int32) * ts
    expert_at = lambda row: jnp.minimum(jnp.sum((row[..., None] >= ends).astype(jnp.int32), axis=-1), N_EXPERTS - 1)
    tile_expert = jnp.where(tile_start < total, expert_at(tile_start), expert_at(total - ts))
    regions = jnp.concatenate([ends, padded]).astype(jnp.int32)
    return (pos.astype(jnp.int32).reshape(n // MOE_TOKEN_TILE, 1, 2 * MOE_TOKEN_TILE),
            tile_expert.astype(jnp.int32), (total // ts).reshape(1).astype(jnp.int32), regions)


def _dispatch_kernel(reg_ref, pos_ref, x_ref, xs_hbm, zbuf, sem, zsem):
    tm, ts = MOE_TOKEN_TILE, MOE_ROW_TILE
    n_rows = xs_hbm.shape[0]

    @pl.when(pl.program_id(0) == 0)
    def _zero_fill():
        zbuf[...] = jnp.zeros_like(zbuf)
        total = reg_ref[N_EXPERTS - 1]
        zero_copy = lambda start: pltpu.make_async_copy(zbuf, xs_hbm.at[pl.ds(pl.multiple_of(start, ts), ts), :], zsem)
        for wait in (False, True):
            for e in range(N_EXPERTS):
                @pl.when(reg_ref[N_EXPERTS + e] > 0)
                def _(e=e, wait=wait):
                    copy = zero_copy(reg_ref[e] - ts)
                    copy.wait() if wait else copy.start()
                @pl.when(total + e * ts < n_rows)
                def _(e=e, wait=wait):
                    copy = zero_copy(total + e * ts)
                    copy.wait() if wait else copy.start()

    def row_copy(r, p):
        return pltpu.make_async_copy(x_ref.at[pl.ds(r, 1), :], xs_hbm.at[pl.ds(p, 1), :], sem)

    def start(r, c):
        row_copy(r, pos_ref[0, 0, 2 * r]).start()
        row_copy(r, pos_ref[0, 0, 2 * r + 1]).start()
        return c
    lax.fori_loop(0, tm, start, 0, unroll=8)

    def wait(r, c):
        row_copy(r, 0).wait()
        row_copy(r, 0).wait()
        return c
    lax.fori_loop(0, tm, wait, 0, unroll=8)


def _dispatch(xn, pos, regions):
    n = xn.shape[0]
    tm, ts = MOE_TOKEN_TILE, MOE_ROW_TILE
    grid_spec = pltpu.PrefetchScalarGridSpec(
        num_scalar_prefetch=1,
        grid=(n // tm,),
        in_specs=[pl.BlockSpec((1, 1, 2 * tm), lambda i, reg: (i, 0, 0), memory_space=pltpu.SMEM),
                  pl.BlockSpec((tm, D_MODEL), lambda i, reg: (i, 0))],
        out_specs=pl.BlockSpec(memory_space=pl.ANY),
        scratch_shapes=[pltpu.VMEM((ts, D_MODEL), F32), pltpu.SemaphoreType.DMA, pltpu.SemaphoreType.DMA],
    )
    return pl.pallas_call(
        _dispatch_kernel,
        grid_spec=grid_spec,
        out_shape=jax.ShapeDtypeStruct((2 * n + N_EXPERTS * ts, D_MODEL), F32),
        compiler_params=_cparams("arbitrary"),
        name="moe_dispatch",
    )(regions, pos, xn)


def _experts_kernel(te_ref, nt_ref, x_ref, wg_ref, wu_ref, wd_ref, o_ref):
    del te_ref

    @pl.when(pl.program_id(1) == 0)
    def _init():
        o_ref[...] = jnp.zeros_like(o_ref)

    @pl.when(pl.program_id(0) < nt_ref[0])
    def _compute():
        xb = x_ref[...].astype(BF16)
        a = jax.nn.silu(_dot(xb, wg_ref[0])) * _dot(xb, wu_ref[0])
        o_ref[...] += _dot(a.astype(BF16), wd_ref[0])


def _experts(xs, wg, wu, wd, tile_expert, n_tiles):
    ts, fc = MOE_ROW_TILE, MOE_F_CHUNK
    tiles = xs.shape[0] // ts
    nf = D_FF_EXPERT // fc
    row = lambda j, nt: jnp.minimum(j, nt[0] - 1)
    col = lambda j, f, nt: jnp.where(j < nt[0], f, nf - 1)
    grid_spec = pltpu.PrefetchScalarGridSpec(
        num_scalar_prefetch=2,
        grid=(tiles, nf),
        in_specs=[pl.BlockSpec((ts, D_MODEL), lambda j, f, te, nt: (row(j, nt), 0)),
                  pl.BlockSpec((1, D_MODEL, fc), lambda j, f, te, nt: (te[j], 0, col(j, f, nt))),
                  pl.BlockSpec((1, D_MODEL, fc), lambda j, f, te, nt: (te[j], 0, col(j, f, nt))),
                  pl.BlockSpec((1, fc, D_MODEL), lambda j, f, te, nt: (te[j], col(j, f, nt), 0))],
        out_specs=pl.BlockSpec((ts, D_MODEL), lambda j, f, te, nt: (j, 0)),
    )
    return pl.pallas_call(
        _experts_kernel,
        grid_spec=grid_spec,
        out_shape=jax.ShapeDtypeStruct(xs.shape, F32),
        compiler_params=_cparams("arbitrary", "arbitrary"),
        name="moe_experts",
    )(tile_expert, n_tiles, xs, wg, wu, wd)


def _combine_kernel(pos_ref, x_ref, gate_ref, g_ref, ys_hbm, o_ref, ybuf, sem, *, final_norm):
    def row_copy(k, r, p):
        return pltpu.make_async_copy(ys_hbm.at[pl.ds(p, 1), :], ybuf.at[k, pl.ds(r, 1), :], sem)

    def start(r, c):
        row_copy(0, r, pos_ref[0, 0, 2 * r]).start()
        row_copy(1, r, pos_ref[0, 0, 2 * r + 1]).start()
        return c
    lax.fori_loop(0, MOE_TOKEN_TILE, start, 0, unroll=8)

    def wait(r, c):
        row_copy(0, r, 0).wait()
        row_copy(1, r, 0).wait()
        return c
    lax.fori_loop(0, MOE_TOKEN_TILE, wait, 0, unroll=8)

    gates = gate_ref[...]
    x = x_ref[...] + (gates[:, 0:1] * ybuf[0] + gates[:, 1:2] * ybuf[1])
    o_ref[...] = _rms(x, g_ref[...]) if final_norm else x


def _combine(x1, ys, pos, gates, g, final_norm):
    n = x1.shape[0]
    tm = MOE_TOKEN_TILE
    tok = lambda w: pl.BlockSpec((tm, w), lambda i: (i, 0))
    return pl.pallas_call(
        functools.partial(_combine_kernel, final_norm=final_norm),
        grid=(n // tm,),
        in_specs=[pl.BlockSpec((1, 1, 2 * tm), lambda i: (i, 0, 0), memory_space=pltpu.SMEM),
                  tok(D_MODEL), tok(128), _resident((1, D_MODEL)), pl.BlockSpec(memory_space=pl.ANY)],
        out_specs=tok(D_MODEL),
        out_shape=jax.ShapeDtypeStruct((n, D_MODEL), F32),
        scratch_shapes=[pltpu.VMEM((2, tm, D_MODEL), F32), pltpu.SemaphoreType.DMA],
        compiler_params=_cparams("arbitrary"),
        name="moe_combine",
    )(pos, x1, gates, g, ys)


def _rope_tables():
    half = B_ROPE // 2
    freqs = ROPE_THETA ** (-jnp.arange(half, dtype=F32) / half)
    ang = jnp.arange(SEQ, dtype=F32)[:, None] * freqs[None]
    cos = jnp.concatenate([jnp.cos(ang), jnp.cos(ang)], axis=-1)
    sin = jnp.concatenate([jnp.sin(ang), jnp.sin(ang)], axis=-1)
    scale = (B_NOPE + B_ROPE) ** -0.5
    ones = jnp.ones((SEQ, B_NOPE), F32)
    zeros_n = jnp.zeros((SEQ, B_NOPE), F32)
    zeros_p = jnp.zeros((SEQ, B_HEAD_LANES - B_NOPE - B_ROPE), F32)
    cq = jnp.tile(jnp.concatenate([ones, cos, zeros_p], axis=-1), (1, B_HEADS)) * scale
    sq = jnp.tile(jnp.concatenate([zeros_n, sin, zeros_p], axis=-1), (1, B_HEADS)) * scale
    rk = jnp.concatenate([cos, sin, jnp.zeros((SEQ, 64), F32)], axis=-1)
    return cq, sq, rk


def _half_swap(w):
    half = B_ROPE // 2
    return jnp.concatenate([-w[..., half:], w[..., :half]], axis=-1)


def _prep_even(w_in, w_uq, w_ukv, w_out):
    scale_a = HEAD_DIM ** -0.5
    k_rope = w_in[:, 3 * A_WIDTH + B_Q_LORA + B_KV_LORA:]
    win = jnp.concatenate([w_in[:, :A_WIDTH] * scale_a, w_in[:, A_WIDTH:3 * A_WIDTH + B_Q_LORA + B_KV_LORA],
                           k_rope, _half_swap(k_rope), jnp.zeros((D_MODEL, 64), F32)], axis=-1).astype(BF16)
    uq = w_uq.reshape(B_Q_LORA, B_HEADS, B_NOPE + B_ROPE)
    pad = jnp.zeros((B_Q_LORA, B_HEADS, B_HEAD_LANES - B_NOPE - B_ROPE), F32)
    wqa = jnp.concatenate([uq, pad], axis=-1).reshape(B_Q_LORA, -1).astype(BF16)
    wqr = jnp.concatenate([jnp.zeros_like(uq[..., :B_NOPE]), _half_swap(uq[..., B_NOPE:]), pad], axis=-1)
    wqr = wqr.reshape(B_Q_LORA, -1).astype(BF16)
    ukv = w_ukv.reshape(B_KV_LORA, B_HEADS, B_NOPE + B_V)
    wk = jnp.concatenate([ukv[..., :B_NOPE], jnp.zeros((B_KV_LORA, B_HEADS, B_HEAD_LANES - B_NOPE), F32)], axis=-1)
    wk = wk.reshape(B_KV_LORA, -1).astype(BF16)
    wv = ukv[..., B_NOPE:].reshape(B_KV_LORA, -1).astype(BF16)
    place = np.zeros((128, B_HEADS, B_HEAD_LANES), np.float32)
    for r in range(B_ROPE):
        place[r, :, B_NOPE + r] = 1.0
    place = jnp.asarray(place.reshape(128, -1), BF16)
    wo = w_out.astype(BF16)
    return win, wqa, wqr, wk, wv, place, wo[:A_WIDTH], wo[A_WIDTH:]


def kernel(x, ln_mix_e, w_in_e, mla_q_norm, mla_w_uq, mla_kv_norm, mla_w_ukv, w_out_e, ln_ffn_e, ffn_w_gate,
           ffn_w_up, ffn_w_down, ln_mix_o, na_w_qkv, na_rpb, na_w_out, ln_ffn_o, moe_router, moe_w_gate,
           moe_w_up, moe_w_down, ln_final):
    batch, seq, d = x.shape
    assert (seq, d) == (SEQ, D_MODEL)
    n = batch * seq
    depth = ln_mix_e.shape[0] + ln_mix_o.shape[0]
    row = lambda v: v.reshape(1, -1).astype(F32)
    cq, sq, rk = _rope_tables()
    bias_a = _dilated_bias()
    xf = x.reshape(n, d)
    for layer in range(depth):
        i = layer // 2
        if layer % 2 == 0:
            win, wqa, wqr, wk, wv, place, woa, wob = _prep_even(w_in_e[i], mla_w_uq[i], mla_w_ukv[i], w_out_e[i])
            qa, ka, va, qb, kb, vb = _front_even(xf, row(ln_mix_e[i]), win, row(mla_q_norm[i]), row(mla_kv_norm[i]),
                                                 wqa, wqr, wk, wv, place, cq, sq, rk, batch)
            oa = _attn_a(qa, ka, va, bias_a, batch)
            ob = _attn_mla(qb, kb, vb, batch)
            xf = _post_even(xf, oa, ob, woa, wob, row(ln_ffn_e[i]), ffn_w_gate[i].astype(BF16),
                            ffn_w_up[i].astype(BF16), ffn_w_down[i].astype(BF16))
        else:
            wqkv = jnp.concatenate([na_w_qkv[i][:, :D_MODEL] * HEAD_DIM ** -0.5, na_w_qkv[i][:, D_MODEL:]], axis=-1)
            q, k, v = _front_odd(xf, row(ln_mix_o[i]), wqkv.astype(BF16))
            att = _attn_na(q, k, v, _na_bias(na_rpb[i]), batch)
            wr = jnp.concatenate([moe_router[i].astype(F32), jnp.zeros((D_MODEL, 128 - N_EXPERTS), F32)], axis=-1)
            x1, xn, idx, gates = _post_odd(xf, att, na_w_out[i].astype(BF16), row(ln_ffn_o[i]), wr)
            pos, tile_expert, n_tiles, regions = _moe_plan(idx[:, :2])
            xs = _dispatch(xn, pos, regions)
            ys = _experts(xs, moe_w_gate[i].astype(BF16), moe_w_up[i].astype(BF16), moe_w_down[i].astype(BF16),
                          tile_expert, n_tiles)
            xf = _combine(x1, ys, pos, gates, row(ln_final), final_norm=layer == depth - 1)
    if depth % 2 == 1:
        raise NotImplementedError("final norm is fused into the last odd layer")
    return xf.reshape(batch, seq, d)
```

```python
import functools
import math

import numpy as np
import jax
import jax.numpy as jnp
from jax import lax
from jax.experimental import pallas as pl
from jax.experimental.pallas import tpu as pltpu

F32 = jnp.float32
BF16 = jnp.bfloat16

D_MODEL = 1024
SEQ = 2048
HEAD_DIM = 64
NORM_EPS = 1e-6
NEG = -1e30
LOG2E = math.log2(math.e)

A_HEADS = 8
A_WIDTH = A_HEADS * HEAD_DIM
A_PATTERNS = ((128, 1), (512, 4), (2048, 16))

B_HEADS = 8
B_Q_LORA = 256
B_KV_LORA = 128
B_NOPE = 64
B_ROPE = 32
B_V = 64
ROPE_THETA = 10000.0
B_HEAD_LANES = 128

C_HEADS = 16
GRID_W = 64
GRID_ROWS = SEQ // GRID_W
NA_ROWS = 8
NA_COLS = 16

D_FF = 2816
N_EXPERTS = 8
D_FF_EXPERT = 3584

VMEM_LIMIT_BYTES = 56 * 1024 * 1024

TOKEN_TILE = 512
Q_TILE = 256
FFN_CHUNK = 1408
MOE_ROW_TILE = 1024
MOE_F_CHUNK = 896
MOE_TOKEN_TILE = 512
NA_Q_ROWS = 4
NA_K_ROWS = 12


def _cparams(*sem):
    return pltpu.CompilerParams(dimension_semantics=sem, vmem_limit_bytes=VMEM_LIMIT_BYTES)


def _rms(x, g):
    return x * lax.rsqrt(jnp.mean(x * x, axis=-1, keepdims=True) + NORM_EPS) * g


def _dot(a, b):
    return jnp.dot(a, b, preferred_element_type=F32)


def _dot_nt(a, b):
    return lax.dot_general(a, b, (((1,), (1,)), ((), ())), preferred_element_type=F32)


def _resident(shape):
    zeros = (0,) * len(shape)
    return pl.BlockSpec(shape, lambda *_: zeros, pipeline_mode=pl.Buffered(1))


def _front_even_kernel(x_ref, g_ref, win_ref, qn_ref, kvn_ref, wqa_ref, wqr_ref, wk_ref, wv_ref, place_ref,
                       cq_ref, sq_ref, rk_ref,
                       qa_ref, ka_ref, va_ref, qb_ref, kb_ref, vb_ref):
    h = _rms(x_ref[...], g_ref[...]).astype(BF16)
    proj = _dot(h, win_ref[...])
    qa_ref[...] = (proj[:, 0:512] * LOG2E).astype(BF16)
    ka_ref[...] = proj[:, 512:1024].astype(BF16)
    va_ref[...] = proj[:, 1024:1536].astype(BF16)

    qn = _rms(proj[:, 1536:1792], qn_ref[...]).astype(BF16)
    q = _dot(qn, wqa_ref[...]) * cq_ref[...] + _dot(qn, wqr_ref[...]) * sq_ref[...]
    qb_ref[...] = q.astype(BF16)

    kvn = _rms(proj[:, 1792:1920], kvn_ref[...]).astype(BF16)
    t = proj[:, 1920:2048] * rk_ref[...]
    k_pe = t + pltpu.roll(t, 96, 1)
    k = _dot(kvn, wk_ref[...]) + _dot(k_pe.astype(BF16), place_ref[...])
    kb_ref[...] = k.astype(BF16)
    vb_ref[...] = _dot(kvn, wv_ref[...]).astype(BF16)


def _front_even(x, g, win, qnorm, kvnorm, wqa, wqr, wk, wv, place, cq, sq, rk, batch):
    n = x.shape[0]
    tm = TOKEN_TILE
    pt = SEQ // tm
    row = lambda p, b: (b * pt + p, 0)
    pos = lambda p, b: (p, 0)
    tok = lambda w: pl.BlockSpec((tm, w), row)
    out = lambda w: jax.ShapeDtypeStruct((n, w), BF16)
    return pl.pallas_call(
        _front_even_kernel,
        grid=(pt, batch),
        in_specs=[tok(D_MODEL), _resident((1, D_MODEL)), _resident(win.shape), _resident((1, B_Q_LORA)),
                  _resident((1, B_KV_LORA)), _resident(wqa.shape), _resident(wqr.shape), _resident(wk.shape),
                  _resident(wv.shape), _resident(place.shape),
                  pl.BlockSpec((tm, 1024), pos), pl.BlockSpec((tm, 1024), pos), pl.BlockSpec((tm, 128), pos)],
        out_specs=[tok(512), tok(512), tok(512), tok(1024), tok(1024), tok(512)],
        out_shape=[out(512), out(512), out(512), out(1024), out(1024), out(512)],
        compiler_params=_cparams("arbitrary", "arbitrary"),
        name="front_even",
    )(x, g, win, qnorm, kvnorm, wqa, wqr, wk, wv, place, cq, sq, rk)


A_TILES = SEQ // Q_TILE


def _head_lane_mask(shape):
    return lax.broadcasted_iota(jnp.int32, shape, 1) < HEAD_DIM


def _fill_value_with_ones(v_ref, vaug_ref):
    v = v_ref[...]
    lo = _head_lane_mask(v.shape)
    one = jnp.ones_like(v)
    vaug_ref[0] = jnp.where(lo, v, one)
    vaug_ref[1] = jnp.where(lo, one, v)


def _normalised_pair(acc0, acc1):
    lo = _head_lane_mask(acc0.shape)
    return jnp.where(lo, acc0 / pltpu.roll(acc0, HEAD_DIM, 1), acc1 / pltpu.roll(acc1, HEAD_DIM, 1))


def _attn_a_kernel(q_ref, k_ref, v_ref, bias_ref, o_ref, vaug_ref):
    _fill_value_with_ones(v_ref, vaug_ref)

    def q_tile(i, carry):
        q0 = pl.multiple_of(i * Q_TILE, Q_TILE)
        q = q_ref[pl.ds(q0, Q_TILE), :]
        lo_mask = _head_lane_mask(q.shape)
        accs = []
        for head in range(2):
            qm = jnp.where(lo_mask if head == 0 else jnp.logical_not(lo_mask), q, jnp.zeros_like(q))
            s = [_dot_nt(qm, k_ref[j * Q_TILE:(j + 1) * Q_TILE, :]) + bias_ref[head, j - i + A_TILES - 1]
                 for j in range(A_TILES)]
            m = functools.reduce(jnp.maximum, [jnp.max(t, axis=-1, keepdims=True) for t in s])
            accs.append(functools.reduce(jnp.add, [
                _dot(jnp.exp2(s[j] - m).astype(BF16), vaug_ref[head, j * Q_TILE:(j + 1) * Q_TILE, :])
                for j in range(A_TILES)]))
        o_ref[pl.ds(q0, Q_TILE), :] = _normalised_pair(*accs).astype(o_ref.dtype)
        return carry

    lax.fori_loop(0, A_TILES, q_tile, 0, unroll=2)


def _attn_a(qa, ka, va, bias, batch):
    n = qa.shape[0]
    blk = pl.BlockSpec((SEQ, 2 * HEAD_DIM), lambda hp, b: (b, hp))
    return pl.pallas_call(
        _attn_a_kernel,
        grid=(A_HEADS // 2, batch),
        in_specs=[blk, blk, blk,
                  pl.BlockSpec((2, 2 * A_TILES - 1, Q_TILE, Q_TILE), lambda hp, b: (hp, 0, 0, 0))],
        out_specs=blk,
        out_shape=jax.ShapeDtypeStruct((n, A_WIDTH), BF16),
        scratch_shapes=[pltpu.VMEM((2, SEQ, 2 * HEAD_DIM), BF16)],
        compiler_params=_cparams("arbitrary", "arbitrary"),
        name="attn_dilated",
    )(qa, ka, va, bias)


def _dilated_bias():
    o = np.arange(-(A_TILES - 1), A_TILES)[:, None, None]
    d = Q_TILE * o + np.arange(Q_TILE)[None, None, :] - np.arange(Q_TILE)[None, :, None]
    ad = np.abs(d)
    mult = np.zeros(d.shape, np.int32)
    for window, dilation in A_PATTERNS:
        mult += ((d % dilation == 0) & (ad <= window // 2)).astype(np.int32)
    logm = np.where(mult > 0, np.log(np.maximum(mult, 1).astype(np.float64)), 0.0).astype(np.float32)
    slopes = np.array([2.0 ** (-8.0 * (h + 1) / A_HEADS) for h in range(A_HEADS)], np.float32)
    bias = -jnp.asarray(slopes)[:, None, None, None] * jnp.asarray(ad.astype(np.float32))[None] + jnp.asarray(logm)[None]
    return jnp.where(jnp.asarray(mult > 0)[None], bias * LOG2E, NEG)


def _attn_mla_kernel(q_ref, k_ref, v_ref, o_ref, vaug_ref):
    _fill_value_with_ones(v_ref, vaug_ref)

    def q_tile(i, carry):
        q0 = pl.multiple_of(i * Q_TILE, Q_TILE)
        accs = []
        for head in range(2):
            lanes = slice(head * B_HEAD_LANES, (head + 1) * B_HEAD_LANES)
            s = _dot_nt(q_ref[pl.ds(q0, Q_TILE), lanes], k_ref[:, lanes])
            p = jnp.exp2(s - jnp.max(s, axis=-1, keepdims=True))
            accs.append(_dot(p.astype(BF16), vaug_ref[head]))
        o_ref[pl.ds(q0, Q_TILE), :] = _normalised_pair(*accs).astype(o_ref.dtype)
        return carry

    lax.fori_loop(0, SEQ // Q_TILE, q_tile, 0, unroll=2)


def _attn_mla(qb, kb, vb, batch):
    n = qb.shape[0]
    qk = pl.BlockSpec((SEQ, 2 * B_HEAD_LANES), lambda hp, b: (b, hp))
    vo = pl.BlockSpec((SEQ, 2 * B_V), lambda hp, b: (b, hp))
    return pl.pallas_call(
        _attn_mla_kernel,
        grid=(B_HEADS // 2, batch),
        in_specs=[qk, qk, vo],
        out_specs=vo,
        out_shape=jax.ShapeDtypeStruct((n, B_HEADS * B_V), BF16),
        scratch_shapes=[pltpu.VMEM((2, SEQ, 2 * B_V), BF16)],
        compiler_params=_cparams("arbitrary", "arbitrary"),
        name="attn_latent",
    )(qb, kb, vb)


def _post_even_kernel(x_ref, oa_ref, ob_ref, woa_ref, wob_ref, g_ref, wg_ref, wu_ref, wd_ref, o_ref):
    x1 = x_ref[...] + _dot(oa_ref[...], woa_ref[...]) + _dot(ob_ref[...], wob_ref[...])
    h = _rms(x1, g_ref[...]).astype(BF16)
    y = None
    for c in range(D_FF // FFN_CHUNK):
        cols = slice(c * FFN_CHUNK, (c + 1) * FFN_CHUNK)
        a = jax.nn.silu(_dot(h, wg_ref[:, cols])) * _dot(h, wu_ref[:, cols])
        d = _dot(a.astype(BF16), wd_ref[cols, :])
        y = d if y is None else y + d
    o_ref[...] = x1 + y


def _post_even(x, oa, ob, woa, wob, g, wg, wu, wd):
    n = x.shape[0]
    tm = TOKEN_TILE
    tok = lambda w: pl.BlockSpec((tm, w), lambda i: (i, 0))
    return pl.pallas_call(
        _post_even_kernel,
        grid=(n // tm,),
        in_specs=[tok(D_MODEL), tok(512), tok(512), _resident(woa.shape), _resident(wob.shape),
                  _resident((1, D_MODEL)), _resident(wg.shape), _resident(wu.shape), _resident(wd.shape)],
        out_specs=tok(D_MODEL),
        out_shape=jax.ShapeDtypeStruct((n, D_MODEL), F32),
        compiler_params=_cparams("arbitrary"),
        name="post_even",
    )(x, oa, ob, woa, wob, g, wg, wu, wd)


def _front_odd_kernel(x_ref, g_ref, w_ref, q_ref, k_ref, v_ref):
    h = _rms(x_ref[...], g_ref[...]).astype(BF16)
    qkv = _dot(h, w_ref[...])
    q_ref[...] = (qkv[:, 0:1024] * LOG2E).astype(BF16)
    k_ref[...] = qkv[:, 1024:2048].astype(BF16)
    v_ref[...] = qkv[:, 2048:3072].astype(BF16)


def _front_odd(x, g, w):
    n = x.shape[0]
    tm = TOKEN_TILE
    tok = pl.BlockSpec((tm, D_MODEL), lambda i: (i, 0))
    out = jax.ShapeDtypeStruct((n, D_MODEL), BF16)
    return pl.pallas_call(
        _front_odd_kernel,
        grid=(n // tm,),
        in_specs=[tok, _resident((1, D_MODEL)), _resident(w.shape)],
        out_specs=[tok, tok, tok],
        out_shape=[out, out, out],
        compiler_params=_cparams("arbitrary"),
        name="front_odd",
    )(x, g, w)


NA_Q = NA_Q_ROWS * GRID_W
NA_K = NA_K_ROWS * GRID_W
NA_BLOCKS = GRID_ROWS // NA_Q_ROWS


def _na_key_row_start(block):
    return np.clip(NA_Q_ROWS * block - NA_ROWS // 2, 0, GRID_ROWS - NA_K_ROWS)


def _attn_na_kernel(q_ref, k_ref, v_ref, bias_ref, o_ref, vaug_ref):
    _fill_value_with_ones(v_ref, vaug_ref)

    def q_block(bi, carry):
        q0 = pl.multiple_of(bi * NA_Q, NA_Q)
        ks = jnp.clip(NA_Q_ROWS * bi - NA_ROWS // 2, 0, GRID_ROWS - NA_K_ROWS)
        k0 = pl.multiple_of(ks * GRID_W, GRID_W)
        case = jnp.where(bi == 0, 0, jnp.where(bi == NA_BLOCKS - 1, 2, 1))
        q = q_ref[pl.ds(q0, NA_Q), :]
        k = k_ref[pl.ds(k0, NA_K), :]
        lo_mask = _head_lane_mask(q.shape)
        accs = []
        for head in range(2):
            qm = jnp.where(lo_mask if head == 0 else jnp.logical_not(lo_mask), q, jnp.zeros_like(q))
            s = _dot_nt(qm, k) + bias_ref[head, case]
            p = jnp.exp2(s - jnp.max(s, axis=-1, keepdims=True))
            accs.append(_dot(p.astype(BF16), vaug_ref[head, pl.ds(k0, NA_K), :]))
        o_ref[pl.ds(q0, NA_Q), :] = _normalised_pair(*accs).astype(o_ref.dtype)
        return carry

    lax.fori_loop(0, NA_BLOCKS, q_block, 0, unroll=True)


def _attn_na(q, k, v, bias, batch):
    n = q.shape[0]
    blk = pl.BlockSpec((SEQ, 2 * HEAD_DIM), lambda hp, b: (b, hp))
    return pl.pallas_call(
        _attn_na_kernel,
        grid=(C_HEADS // 2, batch),
        in_specs=[blk, blk, blk, pl.BlockSpec((2, 3, NA_Q, NA_K), lambda hp, b: (hp, 0, 0, 0))],
        out_specs=blk,
        out_shape=jax.ShapeDtypeStruct((n, C_HEADS * HEAD_DIM), BF16),
        scratch_shapes=[pltpu.VMEM((2, SEQ, 2 * HEAD_DIM), BF16)],
        compiler_params=_cparams("arbitrary", "arbitrary"),
        name="attn_neighbourhood",
    )(q, k, v, bias)


def _na_bias(rpb):
    row_sel, row_ok = [], []
    for block in (0, 1, NA_BLOCKS - 1):
        qi = NA_Q_ROWS * block + np.arange(NA_Q_ROWS)[:, None]
        kr = _na_key_row_start(block) + np.arange(NA_K_ROWS)[None, :]
        rs = np.clip(qi - NA_ROWS // 2, 0, GRID_ROWS - NA_ROWS)
        row_ok.append((kr >= rs) & (kr < rs + NA_ROWS))
        row_sel.append(np.eye(2 * NA_ROWS - 1, dtype=np.float32)[np.clip(kr - qi + NA_ROWS - 1, 0, 2 * NA_ROWS - 2)])
    qc = np.arange(GRID_W)[:, None]
    kc = np.arange(GRID_W)[None, :]
    cs = np.clip(qc - NA_COLS // 2, 0, GRID_W - NA_COLS)
    col_ok = (kc >= cs) & (kc < cs + NA_COLS)
    col_sel = np.eye(2 * NA_COLS - 1, dtype=np.float32)[np.clip(kc - qc + NA_COLS - 1, 0, 2 * NA_COLS - 2)]
    row_sel, row_ok = np.stack(row_sel), np.stack(row_ok)
    table = jnp.einsum("cqka,hab,xyb->hcqxky", jnp.asarray(row_sel), rpb.astype(F32), jnp.asarray(col_sel),
                       precision=lax.Precision.HIGHEST)
    valid = row_ok[:, :, None, :, None] & col_ok[None, None, :, None, :]
    return jnp.where(jnp.asarray(valid)[None], table * LOG2E, NEG).reshape(rpb.shape[0], 3, NA_Q, NA_K)


def _post_odd_kernel(x_ref, att_ref, wo_ref, g_ref, wrh_ref, wrl_ref, x1_ref, xn_ref, idx_ref, gate_ref):
    x1 = x_ref[...] + _dot(att_ref[...], wo_ref[...])
    x1_ref[...] = x1
    xn = _rms(x1, g_ref[...])
    xn_ref[...] = xn
    xh = xn.astype(BF16)
    xl = (xn - xh.astype(F32)).astype(BF16)
    logits = (_dot(xh, wrh_ref[...]) + _dot(xl, wrh_ref[...])) + _dot(xh, wrl_ref[...])
    lane = lax.broadcasted_iota(jnp.int32, logits.shape, 1)
    logits = jnp.where(lane < N_EXPERTS, logits, -jnp.inf)
    m1 = jnp.max(logits, axis=-1, keepdims=True)
    i1 = jnp.min(jnp.where(logits == m1, lane, 128), axis=-1, keepdims=True)
    rest = jnp.where(lane == i1, -jnp.inf, logits)
    m2 = jnp.max(rest, axis=-1, keepdims=True)
    i2 = jnp.min(jnp.where(rest == m2, lane, 128), axis=-1, keepdims=True)
    e2 = jnp.exp(m2 - m1)
    w1 = 1.0 / (1.0 + e2)
    w2 = e2 / (1.0 + e2)
    idx_ref[...] = jnp.where(lane == 0, i1, jnp.where(lane == 1, i2, 0))
    gate_ref[...] = jnp.where(lane == 0, w1, jnp.where(lane == 1, w2, 0.0))


def _post_odd(x, att, wo, g, wr):
    wrh = wr.astype(BF16)
    wrl = (wr - wrh.astype(F32)).astype(BF16)
    n = x.shape[0]
    tm = TOKEN_TILE
    tok = lambda w: pl.BlockSpec((tm, w), lambda i: (i, 0))
    return pl.pallas_call(
        _post_odd_kernel,
        grid=(n // tm,),
        in_specs=[tok(D_MODEL), tok(D_MODEL), _resident(wo.shape), _resident((1, D_MODEL)), _resident(wr.shape),
                  _resident(wr.shape)],
        out_specs=[tok(D_MODEL), tok(D_MODEL), tok(128), tok(128)],
        out_shape=[jax.ShapeDtypeStruct((n, D_MODEL), F32), jax.ShapeDtypeStruct((n, D_MODEL), F32),
                   jax.ShapeDtypeStruct((n, 128), jnp.int32), jax.ShapeDtypeStruct((n, 128), F32)],
        compiler_params=_cparams("arbitrary"),
        name="post_odd_router",
    )(x, att, wo, g, wrh, wrl)


def _moe_plan(expert_idx):
    n = expert_idx.shape[0]
    ts = MOE_ROW_TILE
    tiles = 2 * n // ts + N_EXPERTS
    flat = expert_idx.reshape(-1)
    onehot = (flat[:, None] == jnp.arange(N_EXPERTS, dtype=jnp.int32)[None]).astype(jnp.int32)
    before = jnp.cumsum(onehot, axis=0) - onehot
    counts = jnp.sum(onehot, axis=0)
    padded = (counts + ts - 1) // ts * ts
    ends = jnp.cumsum(padded)
    starts = ends - padded
    pos = jnp.sum(onehot * (starts[None] + before), axis=1)
    total = ends[-1]
    tile_start = jnp.arange(tiles, dtype=jnp.int32) * ts
    expert_at = lambda row: jnp.minimum(jnp.sum((row[..., None] >= ends).astype(jnp.int32), axis=-1), N_EXPERTS - 1)
    tile_expert = jnp.where(tile_start < total, expert_at(tile_start), expert_at(total - ts))
    regions = jnp.concatenate([ends, padded]).astype(jnp.int32)
    return (pos.astype(jnp.int32).reshape(n // MOE_TOKEN_TILE, 1, 2 * MOE_TOKEN_TILE),
            tile_expert.astype(jnp.int32), (total // ts).reshape(1).astype(jnp.int32), regions)


def _dispatch_kernel(reg_ref, pos_ref, x_ref, xs_hbm, zbuf, sem, zsem):
    tm, ts = MOE_TOKEN_TILE, MOE_ROW_TILE
    n_rows = xs_hbm.shape[0]

    @pl.when(pl.program_id(0) == 0)
    def _zero_fill():
        zbuf[...] = jnp.zeros_like(zbuf)
        total = reg_ref[N_EXPERTS - 1]
        zero_copy = lambda start: pltpu.make_async_copy(zbuf, xs_hbm.at[pl.ds(pl.multiple_of(start, ts), ts), :], zsem)
        for wait in (False, True):
            for e in range(N_EXPERTS):
                @pl.when(reg_ref[N_EXPERTS + e] > 0)
                def _(e=e, wait=wait):
                    copy = zero_copy(reg_ref[e] - ts)
                    copy.wait() if wait else copy.start()
                @pl.when(total + e * ts < n_rows)
                def _(e=e, wait=wait):
                    copy = zero_copy(total + e * ts)
                    copy.wait() if wait else copy.start()

    def row_copy(r, p):
        return pltpu.make_async_copy(x_ref.at[pl.ds(r, 1), :], xs_hbm.at[pl.ds(p, 1), :], sem)

    def start(r, c):
        row_copy(r, pos_ref[0, 0, 2 * r]).start(priority=0)
        row_copy(r, pos_ref[0, 0, 2 * r + 1]).start(priority=1)
        return c
    lax.fori_loop(0, tm, start, 0, unroll=8)

    def wait(r, c):
        row_copy(r, 0).wait()
        row_copy(r, 0).wait()
        return c
    lax.fori_loop(0, tm, wait, 0, unroll=8)


def _dispatch(xn, pos, regions):
    n = xn.shape[0]
    tm, ts = MOE_TOKEN_TILE, MOE_ROW_TILE
    grid_spec = pltpu.PrefetchScalarGridSpec(
        num_scalar_prefetch=1,
        grid=(n // tm,),
        in_specs=[pl.BlockSpec((1, 1, 2 * tm), lambda i, reg: (i, 0, 0), memory_space=pltpu.SMEM),
                  pl.BlockSpec((tm, D_MODEL), lambda i, reg: (i, 0))],
        out_specs=pl.BlockSpec(memory_space=pl.ANY),
        scratch_shapes=[pltpu.VMEM((ts, D_MODEL), F32), pltpu.SemaphoreType.DMA, pltpu.SemaphoreType.DMA],
    )
    return pl.pallas_call(
        _dispatch_kernel,
        grid_spec=grid_spec,
        out_shape=jax.ShapeDtypeStruct((2 * n + N_EXPERTS * ts, D_MODEL), F32),
        compiler_params=_cparams("arbitrary"),
        name="moe_dispatch",
    )(regions, pos, xn)


def _experts_kernel(te_ref, nt_ref, x_ref, wg_ref, wu_ref, wd_ref, o_ref):
    del te_ref

    @pl.when(pl.program_id(1) == 0)
    def _init():
        o_ref[...] = jnp.zeros_like(o_ref)

    @pl.when(pl.program_id(0) < nt_ref[0])
    def _compute():
        xb = x_ref[...].astype(BF16)
        a = jax.nn.silu(_dot(xb, wg_ref[0])) * _dot(xb, wu_ref[0])
        o_ref[...] += _dot(a.astype(BF16), wd_ref[0])


def _experts(xs, wg, wu, wd, tile_expert, n_tiles):
    ts, fc = MOE_ROW_TILE, MOE_F_CHUNK
    tiles = xs.shape[0] // ts
    nf = D_FF_EXPERT // fc
    row = lambda j, nt: jnp.minimum(j, nt[0] - 1)
    col = lambda j, f, nt: jnp.where(j < nt[0], f, nf - 1)
    grid_spec = pltpu.PrefetchScalarGridSpec(
        num_scalar_prefetch=2,
        grid=(tiles, nf),
        in_specs=[pl.BlockSpec((ts, D_MODEL), lambda j, f, te, nt: (row(j, nt), 0)),
                  pl.BlockSpec((1, D_MODEL, fc), lambda j, f, te, nt: (te[j], 0, col(j, f, nt))),
                  pl.BlockSpec((1, D_MODEL, fc), lambda j, f, te, nt: (te[j], 0, col(j, f, nt))),
                  pl.BlockSpec((1, fc, D_MODEL), lambda j, f, te, nt: (te[j], col(j, f, nt), 0))],
        out_specs=pl.BlockSpec((ts, D_MODEL), lambda j, f, te, nt: (j, 0)),
    )
    return pl.pallas_call(
        _experts_kernel,
        grid_spec=grid_spec,
        out_shape=jax.ShapeDtypeStruct(xs.shape, F32),
        compiler_params=_cparams("arbitrary", "arbitrary"),
        name="moe_experts",
    )(tile_expert, n_tiles, xs, wg, wu, wd)


def _combine_kernel(pos_ref, x_ref, gate_ref, g_ref, ys_hbm, o_ref, ybuf, sem, *, final_norm):
    def row_copy(k, r, p):
        return pltpu.make_async_copy(ys_hbm.at[pl.ds(p, 1), :], ybuf.at[k, pl.ds(r, 1), :], sem)

    def start(r, c):
        row_copy(0, r, pos_ref[0, 0, 2 * r]).start(priority=0)
        row_copy(1, r, pos_ref[0, 0, 2 * r + 1]).start(priority=1)
        return c
    lax.fori_loop(0, MOE_TOKEN_TILE, start, 0, unroll=8)

    def wait(r, c):
        row_copy(0, r, 0).wait()
        row_copy(1, r, 0).wait()
        return c
    lax.fori_loop(0, MOE_TOKEN_TILE, wait, 0, unroll=8)

    gates = gate_ref[...]
    x = x_ref[...] + (gates[:, 0:1] * ybuf[0] + gates[:, 1:2] * ybuf[1])
    o_ref[...] = _rms(x, g_ref[...]) if final_norm else x


def _combine(x1, ys, pos, gates, g, final_norm):
    n = x1.shape[0]
    tm = MOE_TOKEN_TILE
    tok = lambda w: pl.BlockSpec((tm, w), lambda i: (i, 0))
    return pl.pallas_call(
        functools.partial(_combine_kernel, final_norm=final_norm),
        grid=(n // tm,),
        in_specs=[pl.BlockSpec((1, 1, 2 * tm), lambda i: (i, 0, 0), memory_space=pltpu.SMEM),
                  tok(D_MODEL), tok(128), _resident((1, D_MODEL)), pl.BlockSpec(memory_space=pl.ANY)],
        out_specs=tok(D_MODEL),
        out_shape=jax.ShapeDtypeStruct((n, D_MODEL), F32),
        scratch_shapes=[pltpu.VMEM((2, tm, D_MODEL), F32), pltpu.SemaphoreType.DMA],
        compiler_params=_cparams("arbitrary"),
        name="moe_combine",
    )(pos, x1, gates, g, ys)


def _rope_tables():
    half = B_ROPE // 2
    freqs = ROPE_THETA ** (-jnp.arange(half, dtype=F32) / half)
    ang = jnp.arange(SEQ, dtype=F32)[:, None] * freqs[None]
    cos = jnp.concatenate([jnp.cos(ang), jnp.cos(ang)], axis=-1)
    sin = jnp.concatenate([jnp.sin(ang), jnp.sin(ang)], axis=-1)
    scale = (B_NOPE + B_ROPE) ** -0.5 * LOG2E
    ones = jnp.ones((SEQ, B_NOPE), F32)
    zeros_n = jnp.zeros((SEQ, B_NOPE), F32)
    zeros_p = jnp.zeros((SEQ, B_HEAD_LANES - B_NOPE - B_ROPE), F32)
    cq = jnp.tile(jnp.concatenate([ones, cos, zeros_p], axis=-1), (1, B_HEADS)) * scale
    sq = jnp.tile(jnp.concatenate([zeros_n, sin, zeros_p], axis=-1), (1, B_HEADS)) * scale
    rk = jnp.concatenate([cos, sin, jnp.zeros((SEQ, 64), F32)], axis=-1)
    return cq, sq, rk


def _half_swap(w):
    half = B_ROPE // 2
    return jnp.concatenate([-w[..., half:], w[..., :half]], axis=-1)


def _prep_even(w_in, w_uq, w_ukv, w_out):
    scale_a = HEAD_DIM ** -0.5
    k_rope = w_in[:, 3 * A_WIDTH + B_Q_LORA + B_KV_LORA:]
    win = jnp.concatenate([w_in[:, :A_WIDTH] * scale_a, w_in[:, A_WIDTH:3 * A_WIDTH + B_Q_LORA + B_KV_LORA],
                           k_rope, _half_swap(k_rope), jnp.zeros((D_MODEL, 64), F32)], axis=-1).astype(BF16)
    uq = w_uq.reshape(B_Q_LORA, B_HEADS, B_NOPE + B_ROPE)
    pad = jnp.zeros((B_Q_LORA, B_HEADS, B_HEAD_LANES - B_NOPE - B_ROPE), F32)
    wqa = jnp.concatenate([uq, pad], axis=-1).reshape(B_Q_LORA, -1).astype(BF16)
    wqr = jnp.concatenate([jnp.zeros_like(uq[..., :B_NOPE]), _half_swap(uq[..., B_NOPE:]), pad], axis=-1)
    wqr = wqr.reshape(B_Q_LORA, -1).astype(BF16)
    ukv = w_ukv.reshape(B_KV_LORA, B_HEADS, B_NOPE + B_V)
    wk = jnp.concatenate([ukv[..., :B_NOPE], jnp.zeros((B_KV_LORA, B_HEADS, B_HEAD_LANES - B_NOPE), F32)], axis=-1)
    wk = wk.reshape(B_KV_LORA, -1).astype(BF16)
    wv = ukv[..., B_NOPE:].reshape(B_KV_LORA, -1).astype(BF16)
    place = np.zeros((128, B_HEADS, B_HEAD_LANES), np.float32)
    for r in range(B_ROPE):
        place[r, :, B_NOPE + r] = 1.0
    place = jnp.asarray(place.reshape(128, -1), BF16)
    wo = w_out.astype(BF16)
    return win, wqa, wqr, wk, wv, place, wo[:A_WIDTH], wo[A_WIDTH:]


def kernel(x, ln_mix_e, w_in_e, mla_q_norm, mla_w_uq, mla_kv_norm, mla_w_ukv, w_out_e, ln_ffn_e, ffn_w_gate,
           ffn_w_up, ffn_w_down, ln_mix_o, na_w_qkv, na_rpb, na_w_out, ln_ffn_o, moe_router, moe_w_gate,
           moe_w_up, moe_w_down, ln_final):
    batch, seq, d = x.shape
    assert (seq, d) == (SEQ, D_MODEL)
    n = batch * seq
    depth = ln_mix_e.shape[0] + ln_mix_o.shape[0]
    row = lambda v: v.reshape(1, -1).astype(F32)
    cq, sq, rk = _rope_tables()
    bias_a = _dilated_bias()
    xf = x.reshape(n, d)
    for layer in range(depth):
        i = layer // 2
        if layer % 2 == 0:
            win, wqa, wqr, wk, wv, place, woa, wob = _prep_even(w_in_e[i], mla_w_uq[i], mla_w_ukv[i], w_out_e[i])
            qa, ka, va, qb, kb, vb = _front_even(xf, row(ln_mix_e[i]), win, row(mla_q_norm[i]), row(mla_kv_norm[i]),
                                                 wqa, wqr, wk, wv, place, cq, sq, rk, batch)
            oa = _attn_a(qa, ka, va, bias_a, batch)
            ob = _attn_mla(qb, kb, vb, batch)
            xf = _post_even(xf, oa, ob, woa, wob, row(ln_ffn_e[i]), ffn_w_gate[i].astype(BF16),
                            ffn_w_up[i].astype(BF16), ffn_w_down[i].astype(BF16))
        else:
            wqkv = jnp.concatenate([na_w_qkv[i][:, :D_MODEL] * HEAD_DIM ** -0.5, na_w_qkv[i][:, D_MODEL:]], axis=-1)
            q, k, v = _front_odd(xf, row(ln_mix_o[i]), wqkv.astype(BF16))
            att = _attn_na(q, k, v, _na_bias(na_rpb[i]), batch)
            wr = jnp.concatenate([moe_router[i].astype(F32), jnp.zeros((D_MODEL, 128 - N_EXPERTS), F32)], axis=-1)
            x1, xn, idx, gates = _post_odd(xf, att, na_w_out[i].astype(BF16), row(ln_ffn_o[i]), wr)
            pos, tile_expert, n_tiles, regions = _moe_plan(idx[:, :2])
            xs = _dispatch(xn, pos, regions)
            ys = _experts(xs, moe_w_gate[i].astype(BF16), moe_w_up[i].astype(BF16), moe_w_down[i].astype(BF16),
                          tile_expert, n_tiles)
            xf = _combine(x1, ys, pos, gates, row(ln_final), final_norm=layer == depth - 1)
    if depth % 2 == 1:
        raise NotImplementedError("final norm is fused into the last odd layer")
    return xf.reshape(batch, seq, d)
```

```python
import functools
import math

import numpy as np
import jax
import jax.numpy as jnp
from jax import lax
from jax.experimental import pallas as pl
from jax.experimental.pallas import tpu as pltpu

F32 = jnp.float32
BF16 = jnp.bfloat16

D_MODEL = 1024
SEQ = 2048
HEAD_DIM = 64
NORM_EPS = 1e-6
NEG = -1e30
LOG2E = math.log2(math.e)

A_HEADS = 8
A_WIDTH = A_HEADS * HEAD_DIM
A_PATTERNS = ((128, 1), (512, 4), (2048, 16))

B_HEADS = 8
B_Q_LORA = 256
B_KV_LORA = 128
B_NOPE = 64
B_ROPE = 32
B_V = 64
ROPE_THETA = 10000.0
B_HEAD_LANES = 128

C_HEADS = 16
GRID_W = 64
GRID_ROWS = SEQ // GRID_W
NA_ROWS = 8
NA_COLS = 16

D_FF = 2816
N_EXPERTS = 8
D_FF_EXPERT = 3584

VMEM_LIMIT_BYTES = 56 * 1024 * 1024

TOKEN_TILE = 512
Q_TILE = 256
FFN_CHUNK = 1408
MOE_ROW_TILE = 1024
MOE_F_CHUNK = 512
MOE_TOKEN_TILE = 512
NA_Q_ROWS = 4
NA_K_ROWS = 12


def _cparams(*sem):
    return pltpu.CompilerParams(dimension_semantics=sem, vmem_limit_bytes=VMEM_LIMIT_BYTES)


def _rms(x, g):
    return x * lax.rsqrt(jnp.mean(x * x, axis=-1, keepdims=True) + NORM_EPS) * g


def _dot(a, b):
    return jnp.dot(a, b, preferred_element_type=F32)


def _dot_nt(a, b):
    return lax.dot_general(a, b, (((1,), (1,)), ((), ())), preferred_element_type=F32)


def _resident(shape):
    zeros = (0,) * len(shape)
    return pl.BlockSpec(shape, lambda *_: zeros, pipeline_mode=pl.Buffered(1))


def _front_even_kernel(x_ref, g_ref, win_ref, qn_ref, kvn_ref, wqa_ref, wqr_ref, wk_ref, wv_ref, place_ref,
                       cq_ref, sq_ref, rk_ref,
                       qa_ref, ka_ref, va_ref, qb_ref, kb_ref, vb_ref):
    h = _rms(x_ref[...], g_ref[...]).astype(BF16)
    proj = _dot(h, win_ref[...])
    qa_ref[...] = (proj[:, 0:512] * LOG2E).astype(BF16)
    ka_ref[...] = proj[:, 512:1024].astype(BF16)
    va_ref[...] = proj[:, 1024:1536].astype(BF16)

    qn = _rms(proj[:, 1536:1792], qn_ref[...]).astype(BF16)
    q = _dot(qn, wqa_ref[...]) * cq_ref[...] + _dot(qn, wqr_ref[...]) * sq_ref[...]
    qb_ref[...] = q.astype(BF16)

    kvn = _rms(proj[:, 1792:1920], kvn_ref[...]).astype(BF16)
    t = proj[:, 1920:2048] * rk_ref[...]
    k_pe = t + pltpu.roll(t, 96, 1)
    k = _dot(kvn, wk_ref[...]) + _dot(k_pe.astype(BF16), place_ref[...])
    kb_ref[...] = k.astype(BF16)
    vb_ref[...] = _dot(kvn, wv_ref[...]).astype(BF16)


def _front_even(x, g, win, qnorm, kvnorm, wqa, wqr, wk, wv, place, cq, sq, rk, batch):
    n = x.shape[0]
    tm = TOKEN_TILE
    pt = SEQ // tm
    row = lambda p, b: (b * pt + p, 0)
    pos = lambda p, b: (p, 0)
    tok = lambda w: pl.BlockSpec((tm, w), row)
    out = lambda w: jax.ShapeDtypeStruct((n, w), BF16)
    return pl.pallas_call(
        _front_even_kernel,
        grid=(pt, batch),
        in_specs=[tok(D_MODEL), _resident((1, D_MODEL)), _resident(win.shape), _resident((1, B_Q_LORA)),
                  _resident((1, B_KV_LORA)), _resident(wqa.shape), _resident(wqr.shape), _resident(wk.shape),
                  _resident(wv.shape), _resident(place.shape),
                  pl.BlockSpec((tm, 1024), pos), pl.BlockSpec((tm, 1024), pos), pl.BlockSpec((tm, 128), pos)],
        out_specs=[tok(512), tok(512), tok(512), tok(1024), tok(1024), tok(512)],
        out_shape=[out(512), out(512), out(512), out(1024), out(1024), out(512)],
        compiler_params=_cparams("arbitrary", "arbitrary"),
        name="front_even",
    )(x, g, win, qnorm, kvnorm, wqa, wqr, wk, wv, place, cq, sq, rk)


A_TILES = SEQ // Q_TILE


def _head_lane_mask(shape):
    return lax.broadcasted_iota(jnp.int32, shape, 1) < HEAD_DIM


def _fill_value_with_ones(v_ref, vaug_ref):
    v = v_ref[...]
    lo = _head_lane_mask(v.shape)
    one = jnp.ones_like(v)
    vaug_ref[0] = jnp.where(lo, v, one)
    vaug_ref[1] = jnp.where(lo, one, v)


def _normalised_pair(acc0, acc1):
    lo = _head_lane_mask(acc0.shape)
    return jnp.where(lo, acc0 / pltpu.roll(acc0, HEAD_DIM, 1), acc1 / pltpu.roll(acc1, HEAD_DIM, 1))


def _attn_a_kernel(q_ref, k_ref, v_ref, bias_ref, o_ref, vaug_ref):
    _fill_value_with_ones(v_ref, vaug_ref)

    def q_tile(i, carry):
        q0 = pl.multiple_of(i * Q_TILE, Q_TILE)
        q = q_ref[pl.ds(q0, Q_TILE), :]
        lo_mask = _head_lane_mask(q.shape)
        accs = []
        for head in range(2):
            qm = jnp.where(lo_mask if head == 0 else jnp.logical_not(lo_mask), q, jnp.zeros_like(q))
            s = [_dot_nt(qm, k_ref[j * Q_TILE:(j + 1) * Q_TILE, :]) + bias_ref[head, j - i + A_TILES - 1]
                 for j in range(A_TILES)]
            m = functools.reduce(jnp.maximum, [jnp.max(t, axis=-1, keepdims=True) for t in s])
            accs.append(functools.reduce(jnp.add, [
                _dot(jnp.exp2(s[j] - m).astype(BF16), vaug_ref[head, j * Q_TILE:(j + 1) * Q_TILE, :])
                for j in range(A_TILES)]))
        o_ref[pl.ds(q0, Q_TILE), :] = _normalised_pair(*accs).astype(o_ref.dtype)
        return carry

    lax.fori_loop(0, A_TILES, q_tile, 0, unroll=4)


def _attn_a(qa, ka, va, bias, batch):
    n = qa.shape[0]
    blk = pl.BlockSpec((SEQ, 2 * HEAD_DIM), lambda hp, b: (b, hp))
    return pl.pallas_call(
        _attn_a_kernel,
        grid=(A_HEADS // 2, batch),
        in_specs=[blk, blk, blk,
                  pl.BlockSpec((2, 2 * A_TILES - 1, Q_TILE, Q_TILE), lambda hp, b: (hp, 0, 0, 0))],
        out_specs=blk,
        out_shape=jax.ShapeDtypeStruct((n, A_WIDTH), BF16),
        scratch_shapes=[pltpu.VMEM((2, SEQ, 2 * HEAD_DIM), BF16)],
        compiler_params=_cparams("arbitrary", "arbitrary"),
        name="attn_dilated",
    )(qa, ka, va, bias)


def _dilated_bias():
    o = np.arange(-(A_TILES - 1), A_TILES)[:, None, None]
    d = Q_TILE * o + np.arange(Q_TILE)[None, None, :] - np.arange(Q_TILE)[None, :, None]
    ad = np.abs(d)
    mult = np.zeros(d.shape, np.int32)
    for window, dilation in A_PATTERNS:
        mult += ((d % dilation == 0) & (ad <= window // 2)).astype(np.int32)
    logm = np.where(mult > 0, np.log(np.maximum(mult, 1).astype(np.float64)), 0.0).astype(np.float32)
    slopes = np.array([2.0 ** (-8.0 * (h + 1) / A_HEADS) for h in range(A_HEADS)], np.float32)
    bias = -jnp.asarray(slopes)[:, None, None, None] * jnp.asarray(ad.astype(np.float32))[None] + jnp.asarray(logm)[None]
    return jnp.where(jnp.asarray(mult > 0)[None], bias * LOG2E, NEG)


def _attn_mla_kernel(q_ref, k_ref, v_ref, o_ref, vaug_ref):
    _fill_value_with_ones(v_ref, vaug_ref)

    def q_tile(i, carry):
        q0 = pl.multiple_of(i * Q_TILE, Q_TILE)
        accs = []
        for head in range(2):
            lanes = slice(head * B_HEAD_LANES, (head + 1) * B_HEAD_LANES)
            s = _dot_nt(q_ref[pl.ds(q0, Q_TILE), lanes], k_ref[:, lanes])
            p = jnp.exp2(s - jnp.max(s, axis=-1, keepdims=True))
            accs.append(_dot(p.astype(BF16), vaug_ref[head]))
        o_ref[pl.ds(q0, Q_TILE), :] = _normalised_pair(*accs).astype(o_ref.dtype)
        return carry

    lax.fori_loop(0, SEQ // Q_TILE, q_tile, 0, unroll=4)


def _attn_mla(qb, kb, vb, batch):
    n = qb.shape[0]
    qk = pl.BlockSpec((SEQ, 2 * B_HEAD_LANES), lambda hp, b: (b, hp))
    vo = pl.BlockSpec((SEQ, 2 * B_V), lambda hp, b: (b, hp))
    return pl.pallas_call(
        _attn_mla_kernel,
        grid=(B_HEADS // 2, batch),
        in_specs=[qk, qk, vo],
        out_specs=vo,
        out_shape=jax.ShapeDtypeStruct((n, B_HEADS * B_V), BF16),
        scratch_shapes=[pltpu.VMEM((2, SEQ, 2 * B_V), BF16)],
        compiler_params=_cparams("arbitrary", "arbitrary"),
        name="attn_latent",
    )(qb, kb, vb)


def _post_even_kernel(x_ref, oa_ref, ob_ref, woa_ref, wob_ref, g_ref, wg_ref, wu_ref, wd_ref, o_ref):
    x1 = x_ref[...] + _dot(oa_ref[...], woa_ref[...]) + _dot(ob_ref[...], wob_ref[...])
    h = _rms(x1, g_ref[...]).astype(BF16)
    y = None
    for c in range(D_FF // FFN_CHUNK):
        cols = slice(c * FFN_CHUNK, (c + 1) * FFN_CHUNK)
        a = jax.nn.silu(_dot(h, wg_ref[:, cols])) * _dot(h, wu_ref[:, cols])
        d = _dot(a.astype(BF16), wd_ref[cols, :])
        y = d if y is None else y + d
    o_ref[...] = x1 + y


def _post_even(x, oa, ob, woa, wob, g, wg, wu, wd):
    n = x.shape[0]
    tm = TOKEN_TILE
    tok = lambda w: pl.BlockSpec((tm, w), lambda i: (i, 0))
    return pl.pallas_call(
        _post_even_kernel,
        grid=(n // tm,),
        in_specs=[tok(D_MODEL), tok(512), tok(512), _resident(woa.shape), _resident(wob.shape),
                  _resident((1, D_MODEL)), _resident(wg.shape), _resident(wu.shape), _resident(wd.shape)],
        out_specs=tok(D_MODEL),
        out_shape=jax.ShapeDtypeStruct((n, D_MODEL), F32),
        compiler_params=_cparams("arbitrary"),
        name="post_even",
    )(x, oa, ob, woa, wob, g, wg, wu, wd)


def _front_odd_kernel(x_ref, g_ref, w_ref, q_ref, k_ref, v_ref):
    h = _rms(x_ref[...], g_ref[...]).astype(BF16)
    qkv = _dot(h, w_ref[...])
    q_ref[...] = (qkv[:, 0:1024] * LOG2E).astype(BF16)
    k_ref[...] = qkv[:, 1024:2048].astype(BF16)
    v_ref[...] = qkv[:, 2048:3072].astype(BF16)


def _front_odd(x, g, w):
    n = x.shape[0]
    tm = TOKEN_TILE
    tok = pl.BlockSpec((tm, D_MODEL), lambda i: (i, 0))
    out = jax.ShapeDtypeStruct((n, D_MODEL), BF16)
    return pl.pallas_call(
        _front_odd_kernel,
        grid=(n // tm,),
        in_specs=[tok, _resident((1, D_MODEL)), _resident(w.shape)],
        out_specs=[tok, tok, tok],
        out_shape=[out, out, out],
        compiler_params=_cparams("arbitrary"),
        name="front_odd",
    )(x, g, w)


NA_Q = NA_Q_ROWS * GRID_W
NA_K = NA_K_ROWS * GRID_W
NA_BLOCKS = GRID_ROWS // NA_Q_ROWS


def _na_key_row_start(block):
    return np.clip(NA_Q_ROWS * block - NA_ROWS // 2, 0, GRID_ROWS - NA_K_ROWS)


def _attn_na_kernel(q_ref, k_ref, v_ref, bias_ref, o_ref, vaug_ref):
    _fill_value_with_ones(v_ref, vaug_ref)

    def q_block(bi, carry):
        q0 = pl.multiple_of(bi * NA_Q, NA_Q)
        ks = jnp.clip(NA_Q_ROWS * bi - NA_ROWS // 2, 0, GRID_ROWS - NA_K_ROWS)
        k0 = pl.multiple_of(ks * GRID_W, GRID_W)
        case = jnp.where(bi == 0, 0, jnp.where(bi == NA_BLOCKS - 1, 2, 1))
        q = q_ref[pl.ds(q0, NA_Q), :]
        k = k_ref[pl.ds(k0, NA_K), :]
        lo_mask = _head_lane_mask(q.shape)
        accs = []
        for head in range(2):
            qm = jnp.where(lo_mask if head == 0 else jnp.logical_not(lo_mask), q, jnp.zeros_like(q))
            s = _dot_nt(qm, k) + bias_ref[head, case]
            p = jnp.exp2(s - jnp.max(s, axis=-1, keepdims=True))
            accs.append(_dot(p.astype(BF16), vaug_ref[head, pl.ds(k0, NA_K), :]))
        o_ref[pl.ds(q0, NA_Q), :] = _normalised_pair(*accs).astype(o_ref.dtype)
        return carry

    lax.fori_loop(0, NA_BLOCKS, q_block, 0, unroll=True)


def _attn_na(q, k, v, bias, batch):
    n = q.shape[0]
    blk = pl.BlockSpec((SEQ, 2 * HEAD_DIM), lambda hp, b: (b, hp))
    return pl.pallas_call(
        _attn_na_kernel,
        grid=(C_HEADS // 2, batch),
        in_specs=[blk, blk, blk, pl.BlockSpec((2, 3, NA_Q, NA_K), lambda hp, b: (hp, 0, 0, 0))],
        out_specs=blk,
        out_shape=jax.ShapeDtypeStruct((n, C_HEADS * HEAD_DIM), BF16),
        scratch_shapes=[pltpu.VMEM((2, SEQ, 2 * HEAD_DIM), BF16)],
        compiler_params=_cparams("arbitrary", "arbitrary"),
        name="attn_neighbourhood",
    )(q, k, v, bias)


def _na_bias(rpb):
    row_sel, row_ok = [], []
    for block in (0, 1, NA_BLOCKS - 1):
        qi = NA_Q_ROWS * block + np.arange(NA_Q_ROWS)[:, None]
        kr = _na_key_row_start(block) + np.arange(NA_K_ROWS)[None, :]
        rs = np.clip(qi - NA_ROWS // 2, 0, GRID_ROWS - NA_ROWS)
        row_ok.append((kr >= rs) & (kr < rs + NA_ROWS))
        row_sel.append(np.eye(2 * NA_ROWS - 1, dtype=np.float32)[np.clip(kr - qi + NA_ROWS - 1, 0, 2 * NA_ROWS - 2)])
    qc = np.arange(GRID_W)[:, None]
    kc = np.arange(GRID_W)[None, :]
    cs = np.clip(qc - NA_COLS // 2, 0, GRID_W - NA_COLS)
    col_ok = (kc >= cs) & (kc < cs + NA_COLS)
    col_sel = np.eye(2 * NA_COLS - 1, dtype=np.float32)[np.clip(kc - qc + NA_COLS - 1, 0, 2 * NA_COLS - 2)]
    row_sel, row_ok = np.stack(row_sel), np.stack(row_ok)
    table = jnp.einsum("cqka,hab,xyb->hcqxky", jnp.asarray(row_sel), rpb.astype(F32), jnp.asarray(col_sel),
                       precision=lax.Precision.HIGHEST)
    valid = row_ok[:, :, None, :, None] & col_ok[None, None, :, None, :]
    return jnp.where(jnp.asarray(valid)[None], table * LOG2E, NEG).reshape(rpb.shape[0], 3, NA_Q, NA_K)


ROUTER_SPLIT = 2


def _post_odd_kernel(x_ref, att_ref, wo_ref, g_ref, wr_ref, x1_ref, xn_ref, idx_ref, gate_ref):
    rows_per_group = x_ref.shape[0] // ROUTER_SPLIT
    for group in range(ROUTER_SPLIT):
        rows = slice(group * rows_per_group, (group + 1) * rows_per_group)
        x1 = x_ref[rows, :] + _dot(att_ref[rows, :], wo_ref[...])
        x1_ref[rows, :] = x1
        xn = _rms(x1, g_ref[...])
        xn_ref[rows, :] = xn
        xh = xn.astype(BF16)
        xl = (xn - xh.astype(F32)).astype(BF16)
        both = _dot(xh, wr_ref[...])
        logits = (both[:, :128] + _dot(xl, wr_ref[:, :128])) + both[:, 128:]
        lane = lax.broadcasted_iota(jnp.int32, logits.shape, 1)
        logits = jnp.where(lane < N_EXPERTS, logits, -jnp.inf)
        m1 = jnp.max(logits, axis=-1, keepdims=True)
        i1 = jnp.min(jnp.where(logits == m1, lane, 128), axis=-1, keepdims=True)
        rest = jnp.where(lane == i1, -jnp.inf, logits)
        m2 = jnp.max(rest, axis=-1, keepdims=True)
        i2 = jnp.min(jnp.where(rest == m2, lane, 128), axis=-1, keepdims=True)
        e2 = jnp.exp(m2 - m1)
        w1 = 1.0 / (1.0 + e2)
        w2 = e2 / (1.0 + e2)
        idx_ref[rows, :] = jnp.where(lane == 0, i1, jnp.where(lane == 1, i2, 0))
        gate_ref[rows, :] = jnp.where(lane == 0, w1, jnp.where(lane == 1, w2, 0.0))


def _post_odd(x, att, wo, g, wr):
    wrh = wr.astype(BF16)
    wr = jnp.concatenate([wrh, (wr - wrh.astype(F32)).astype(BF16)], axis=-1)
    n = x.shape[0]
    tm = TOKEN_TILE
    tok = lambda w: pl.BlockSpec((tm, w), lambda i: (i, 0))
    return pl.pallas_call(
        _post_odd_kernel,
        grid=(n // tm,),
        in_specs=[tok(D_MODEL), tok(D_MODEL), _resident(wo.shape), _resident((1, D_MODEL)), _resident(wr.shape)],
        out_specs=[tok(D_MODEL), tok(D_MODEL), tok(128), tok(128)],
        out_shape=[jax.ShapeDtypeStruct((n, D_MODEL), F32), jax.ShapeDtypeStruct((n, D_MODEL), F32),
                   jax.ShapeDtypeStruct((n, 128), jnp.int32), jax.ShapeDtypeStruct((n, 128), F32)],
        compiler_params=_cparams("arbitrary"),
        name="post_odd_router",
    )(x, att, wo, g, wr)


def _moe_plan(expert_idx):
    n = expert_idx.shape[0]
    ts = MOE_ROW_TILE
    tiles = 2 * n // ts + N_EXPERTS
    flat = expert_idx.reshape(-1)
    onehot = (flat[:, None] == jnp.arange(N_EXPERTS, dtype=jnp.int32)[None]).astype(jnp.int32)
    before = jnp.cumsum(onehot, axis=0) - onehot
    counts = jnp.sum(onehot, axis=0)
    padded = (counts + ts - 1) // ts * ts
    ends = jnp.cumsum(padded)
    starts = ends - padded
    pos = jnp.sum(onehot * (starts[None] + before), axis=1)
    total = ends[-1]
    tile_start = jnp.arange(tiles, dtype=jnp.int32) * ts
    expert_at = lambda row: jnp.minimum(jnp.sum((row[..., None] >= ends).astype(jnp.int32), axis=-1), N_EXPERTS - 1)
    tile_expert = jnp.where(tile_start < total, expert_at(tile_start), expert_at(total - ts))
    regions = jnp.concatenate([ends, padded]).astype(jnp.int32)
    return (pos.astype(jnp.int32).reshape(n // MOE_TOKEN_TILE, 1, 2 * MOE_TOKEN_TILE),
            tile_expert.astype(jnp.int32), (total // ts).reshape(1).astype(jnp.int32), regions)


def _dispatch_kernel(reg_ref, pos_ref, x_ref, xs_hbm, zbuf, sem, zsem):
    tm, ts = MOE_TOKEN_TILE, MOE_ROW_TILE
    n_rows = xs_hbm.shape[0]

    @pl.when(pl.program_id(0) == 0)
    def _zero_fill():
        zbuf[...] = jnp.zeros_like(zbuf)
        total = reg_ref[N_EXPERTS - 1]
        zero_copy = lambda start: pltpu.make_async_copy(zbuf, xs_hbm.at[pl.ds(pl.multiple_of(start, ts), ts), :], zsem)
        for wait in (False, True):
            for e in range(N_EXPERTS):
                @pl.when(reg_ref[N_EXPERTS + e] > 0)
                def _(e=e, wait=wait):
                    copy = zero_copy(reg_ref[e] - ts)
                    copy.wait() if wait else copy.start()
                @pl.when(total + e * ts < n_rows)
                def _(e=e, wait=wait):
                    copy = zero_copy(total + e * ts)
                    copy.wait() if wait else copy.start()

    def row_copy(r, p):
        return pltpu.make_async_copy(x_ref.at[pl.ds(r, 1), :], xs_hbm.at[pl.ds(p, 1), :], sem)

    def start(r, c):
        row_copy(r, pos_ref[0, 0, 2 * r]).start(priority=0)
        row_copy(r, pos_ref[0, 0, 2 * r + 1]).start(priority=1)
        return c
    lax.fori_loop(0, tm, start, 0, unroll=8)

    def wait(r, c):
        row_copy(r, 0).wait()
        row_copy(r, 0).wait()
        return c
    lax.fori_loop(0, tm, wait, 0, unroll=8)


def _dispatch(xn, pos, regions):
    n = xn.shape[0]
    tm, ts = MOE_TOKEN_TILE, MOE_ROW_TILE
    grid_spec = pltpu.PrefetchScalarGridSpec(
        num_scalar_prefetch=1,
        grid=(n // tm,),
        in_specs=[pl.BlockSpec((1, 1, 2 * tm), lambda i, reg: (i, 0, 0), memory_space=pltpu.SMEM),
                  pl.BlockSpec((tm, D_MODEL), lambda i, reg: (i, 0))],
        out_specs=pl.BlockSpec(memory_space=pl.ANY),
        scratch_shapes=[pltpu.VMEM((ts, D_MODEL), F32), pltpu.SemaphoreType.DMA, pltpu.SemaphoreType.DMA],
    )
    return pl.pallas_call(
        _dispatch_kernel,
        grid_spec=grid_spec,
        out_shape=jax.ShapeDtypeStruct((2 * n + N_EXPERTS * ts, D_MODEL), F32),
        compiler_params=_cparams("arbitrary"),
        name="moe_dispatch",
    )(regions, pos, xn)


def _experts_kernel(te_ref, nt_ref, x_ref, wg_ref, wu_ref, wd_ref, o_ref):
    del te_ref

    @pl.when(pl.program_id(1) == 0)
    def _init():
        o_ref[...] = jnp.zeros_like(o_ref)

    @pl.when(pl.program_id(0) < nt_ref[0])
    def _compute():
        xb = x_ref[...].astype(BF16)
        a = jax.nn.silu(_dot(xb, wg_ref[0])) * _dot(xb, wu_ref[0])
        o_ref[...] += _dot(a.astype(BF16), wd_ref[0])


def _experts(xs, wg, wu, wd, tile_expert, n_tiles):
    ts, fc = MOE_ROW_TILE, MOE_F_CHUNK
    tiles = xs.shape[0] // ts
    nf = D_FF_EXPERT // fc
    row = lambda j, nt: jnp.minimum(j, nt[0] - 1)
    col = lambda j, f, nt: jnp.where(j < nt[0], f, nf - 1)
    grid_spec = pltpu.PrefetchScalarGridSpec(
        num_scalar_prefetch=2,
        grid=(tiles, nf),
        in_specs=[pl.BlockSpec((ts, D_MODEL), lambda j, f, te, nt: (row(j, nt), 0)),
                  pl.BlockSpec((1, D_MODEL, fc), lambda j, f, te, nt: (te[j], 0, col(j, f, nt))),
                  pl.BlockSpec((1, D_MODEL, fc), lambda j, f, te, nt: (te[j], 0, col(j, f, nt))),
                  pl.BlockSpec((1, fc, D_MODEL), lambda j, f, te, nt: (te[j], col(j, f, nt), 0))],
        out_specs=pl.BlockSpec((ts, D_MODEL), lambda j, f, te, nt: (j, 0)),
    )
    return pl.pallas_call(
        _experts_kernel,
        grid_spec=grid_spec,
        out_shape=jax.ShapeDtypeStruct(xs.shape, F32),
        compiler_params=_cparams("arbitrary", "arbitrary"),
        name="moe_experts",
    )(tile_expert, n_tiles, xs, wg, wu, wd)


def _combine_kernel(pos_ref, next_pos_ref, x_ref, gate_ref, g_ref, ys_hbm, o_ref, ybuf, sem, *, final_norm):
    i = pl.program_id(0)
    slot = i % 2

    def row_copy(s, k, r, p):
        return pltpu.make_async_copy(ys_hbm.at[pl.ds(p, 1), :], ybuf.at[s, k, pl.ds(r, 1), :], sem.at[s])

    def fetch(s, tile_pos_ref):
        def start(r, c):
            row_copy(s, 0, r, tile_pos_ref[0, 0, 2 * r]).start(priority=0)
            row_copy(s, 1, r, tile_pos_ref[0, 0, 2 * r + 1]).start(priority=1)
            return c
        lax.fori_loop(0, MOE_TOKEN_TILE, start, 0, unroll=8)

    @pl.when(i == 0)
    def _first():
        fetch(0, pos_ref)

    @pl.when(i + 1 < pl.num_programs(0))
    def _next():
        fetch(1 - slot, next_pos_ref)

    def wait(r, c):
        row_copy(slot, 0, r, 0).wait()
        row_copy(slot, 1, r, 0).wait()
        return c
    lax.fori_loop(0, MOE_TOKEN_TILE, wait, 0, unroll=8)

    gates = gate_ref[...]
    x = x_ref[...] + (gates[:, 0:1] * ybuf[slot, 0] + gates[:, 1:2] * ybuf[slot, 1])
    o_ref[...] = _rms(x, g_ref[...]) if final_norm else x


def _combine(x1, ys, pos, gates, g, final_norm):
    n = x1.shape[0]
    tm = MOE_TOKEN_TILE
    tok = lambda w: pl.BlockSpec((tm, w), lambda i: (i, 0))
    return pl.pallas_call(
        functools.partial(_combine_kernel, final_norm=final_norm),
        grid=(n // tm,),
        in_specs=[pl.BlockSpec((1, 1, 2 * tm), lambda i: (i, 0, 0), memory_space=pltpu.SMEM),
                  pl.BlockSpec((1, 1, 2 * tm), lambda i: (jnp.minimum(i + 1, n // tm - 1), 0, 0),
                               memory_space=pltpu.SMEM),
                  tok(D_MODEL), tok(128), _resident((1, D_MODEL)), pl.BlockSpec(memory_space=pl.ANY)],
        out_specs=tok(D_MODEL),
        out_shape=jax.ShapeDtypeStruct((n, D_MODEL), F32),
        scratch_shapes=[pltpu.VMEM((2, 2, tm, D_MODEL), F32), pltpu.SemaphoreType.DMA((2,))],
        compiler_params=_cparams("arbitrary"),
        name="moe_combine",
    )(pos, pos, x1, gates, g, ys)


def _rope_tables():
    half = B_ROPE // 2
    freqs = ROPE_THETA ** (-jnp.arange(half, dtype=F32) / half)
    ang = jnp.arange(SEQ, dtype=F32)[:, None] * freqs[None]
    cos = jnp.concatenate([jnp.cos(ang), jnp.cos(ang)], axis=-1)
    sin = jnp.concatenate([jnp.sin(ang), jnp.sin(ang)], axis=-1)
    scale = (B_NOPE + B_ROPE) ** -0.5 * LOG2E
    ones = jnp.ones((SEQ, B_NOPE), F32)
    zeros_n = jnp.zeros((SEQ, B_NOPE), F32)
    zeros_p = jnp.zeros((SEQ, B_HEAD_LANES - B_NOPE - B_ROPE), F32)
    cq = jnp.tile(jnp.concatenate([ones, cos, zeros_p], axis=-1), (1, B_HEADS)) * scale
    sq = jnp.tile(jnp.concatenate([zeros_n, sin, zeros_p], axis=-1), (1, B_HEADS)) * scale
    rk = jnp.concatenate([cos, sin, jnp.zeros((SEQ, 64), F32)], axis=-1)
    return cq, sq, rk


def _half_swap(w):
    half = B_ROPE // 2
    return jnp.concatenate([-w[..., half:], w[..., :half]], axis=-1)


def _prep_even(w_in, w_uq, w_ukv, w_out):
    scale_a = HEAD_DIM ** -0.5
    k_rope = w_in[:, 3 * A_WIDTH + B_Q_LORA + B_KV_LORA:]
    win = jnp.concatenate([w_in[:, :A_WIDTH] * scale_a, w_in[:, A_WIDTH:3 * A_WIDTH + B_Q_LORA + B_KV_LORA],
                           k_rope, _half_swap(k_rope), jnp.zeros((D_MODEL, 64), F32)], axis=-1).astype(BF16)
    uq = w_uq.reshape(B_Q_LORA, B_HEADS, B_NOPE + B_ROPE)
    pad = jnp.zeros((B_Q_LORA, B_HEADS, B_HEAD_LANES - B_NOPE - B_ROPE), F32)
    wqa = jnp.concatenate([uq, pad], axis=-1).reshape(B_Q_LORA, -1).astype(BF16)
    wqr = jnp.concatenate([jnp.zeros_like(uq[..., :B_NOPE]), _half_swap(uq[..., B_NOPE:]), pad], axis=-1)
    wqr = wqr.reshape(B_Q_LORA, -1).astype(BF16)
    ukv = w_ukv.reshape(B_KV_LORA, B_HEADS, B_NOPE + B_V)
    wk = jnp.concatenate([ukv[..., :B_NOPE], jnp.zeros((B_KV_LORA, B_HEADS, B_HEAD_LANES - B_NOPE), F32)], axis=-1)
    wk = wk.reshape(B_KV_LORA, -1).astype(BF16)
    wv = ukv[..., B_NOPE:].reshape(B_KV_LORA, -1).astype(BF16)
    place = np.zeros((128, B_HEADS, B_HEAD_LANES), np.float32)
    for r in range(B_ROPE):
        place[r, :, B_NOPE + r] = 1.0
    place = jnp.asarray(place.reshape(128, -1), BF16)
    wo = w_out.astype(BF16)
    return win, wqa, wqr, wk, wv, place, wo[:A_WIDTH], wo[A_WIDTH:]


def kernel(x, ln_mix_e, w_in_e, mla_q_norm, mla_w_uq, mla_kv_norm, mla_w_ukv, w_out_e, ln_ffn_e, ffn_w_gate,
           ffn_w_up, ffn_w_down, ln_mix_o, na_w_qkv, na_rpb, na_w_out, ln_ffn_o, moe_router, moe_w_gate,
           moe_w_up, moe_w_down, ln_final):
    batch, seq, d = x.shape
    assert (seq, d) == (SEQ, D_MODEL)
    n = batch * seq
    depth = ln_mix_e.shape[0] + ln_mix_o.shape[0]
    row = lambda v: v.reshape(1, -1).astype(F32)
    cq, sq, rk = _rope_tables()
    bias_a = _dilated_bias()
    xf = x.reshape(n, d)
    for layer in range(depth):
        i = layer // 2
        if layer % 2 == 0:
            win, wqa, wqr, wk, wv, place, woa, wob = _prep_even(w_in_e[i], mla_w_uq[i], mla_w_ukv[i], w_out_e[i])
            qa, ka, va, qb, kb, vb = _front_even(xf, row(ln_mix_e[i]), win, row(mla_q_norm[i]), row(mla_kv_norm[i]),
                                                 wqa, wqr, wk, wv, place, cq, sq, rk, batch)
            oa = _attn_a(qa, ka, va, bias_a, batch)
            ob = _attn_mla(qb, kb, vb, batch)
            xf = _post_even(xf, oa, ob, woa, wob, row(ln_ffn_e[i]), ffn_w_gate[i].astype(BF16),
                            ffn_w_up[i].astype(BF16), ffn_w_down[i].astype(BF16))
        else:
            wqkv = jnp.concatenate([na_w_qkv[i][:, :D_MODEL] * HEAD_DIM ** -0.5, na_w_qkv[i][:, D_MODEL:]], axis=-1)
            q, k, v = _front_odd(xf, row(ln_mix_o[i]), wqkv.astype(BF16))
            att = _attn_na(q, k, v, _na_bias(na_rpb[i]), batch)
            wr = jnp.concatenate([moe_router[i].astype(F32), jnp.zeros((D_MODEL, 128 - N_EXPERTS), F32)], axis=-1)
            x1, xn, idx, gates = _post_odd(xf, att, na_w_out[i].astype(BF16), row(ln_ffn_o[i]), wr)
            pos, tile_expert, n_tiles, regions = _moe_plan(idx[:, :2])
            xs = _dispatch(xn, pos, regions)
            ys = _experts(xs, moe_w_gate[i].astype(BF16), moe_w_up[i].astype(BF16), moe_w_down[i].astype(BF16),
                          tile_expert, n_tiles)
            xf = _combine(x1, ys, pos, gates, row(ln_final), final_norm=layer == depth - 1)
    if depth % 2 == 1:
        raise NotImplementedError("final norm is fused into the last odd layer")
    return xf.reshape(batch, seq, d)
```

```python
import functools
import math

import numpy as np
import jax
import jax.numpy as jnp
from jax import lax
from jax.experimental import pallas as pl
from jax.experimental.pallas import tpu as pltpu

F32 = jnp.float32
BF16 = jnp.bfloat16

D_MODEL = 1024
SEQ = 2048
HEAD_DIM = 64
NORM_EPS = 1e-6
NEG = -1e30
LOG2E = math.log2(math.e)

A_HEADS = 8
A_WIDTH = A_HEADS * HEAD_DIM
A_PATTERNS = ((128, 1), (512, 4), (2048, 16))

B_HEADS = 8
B_Q_LORA = 256
B_KV_LORA = 128
B_NOPE = 64
B_ROPE = 32
B_V = 64
ROPE_THETA = 10000.0
B_HEAD_LANES = 128

C_HEADS = 16
GRID_W = 64
GRID_ROWS = SEQ // GRID_W
NA_ROWS = 8
NA_COLS = 16

D_FF = 2816
N_EXPERTS = 8
D_FF_EXPERT = 3584

VMEM_LIMIT_BYTES = 56 * 1024 * 1024

TOKEN_TILE = 512
Q_TILE = 256
FFN_CHUNK = 1408
MOE_ROW_TILE = 1024
MOE_F_CHUNK = 512
MOE_TOKEN_TILE = 512
NA_Q_ROWS = 4
NA_K_ROWS = 12


def _cparams(*sem):
    return pltpu.CompilerParams(dimension_semantics=sem, vmem_limit_bytes=VMEM_LIMIT_BYTES)


def _rms(x, g):
    return x * lax.rsqrt(jnp.mean(x * x, axis=-1, keepdims=True) + NORM_EPS) * g


def _dot(a, b):
    return jnp.dot(a, b, preferred_element_type=F32)


def _dot_nt(a, b):
    return lax.dot_general(a, b, (((1,), (1,)), ((), ())), preferred_element_type=F32)


def _resident(shape):
    zeros = (0,) * len(shape)
    return pl.BlockSpec(shape, lambda *_: zeros, pipeline_mode=pl.Buffered(1))


def _front_even_kernel(x_ref, g_ref, win_ref, qn_ref, kvn_ref, wqa_ref, wqr_ref, wk_ref, wv_ref, place_ref,
                       cq_ref, sq_ref, rk_ref,
                       qa_ref, ka_ref, va_ref, qb_ref, kb_ref, vb_ref):
    h = _rms(x_ref[...], g_ref[...]).astype(BF16)
    proj = _dot(h, win_ref[...])
    qa_ref[...] = (proj[:, 0:512] * LOG2E).astype(BF16)
    ka_ref[...] = proj[:, 512:1024].astype(BF16)
    va_ref[...] = proj[:, 1024:1536].astype(BF16)

    qn = _rms(proj[:, 1536:1792], qn_ref[...]).astype(BF16)
    q = _dot(qn, wqa_ref[...]) * cq_ref[...] + _dot(qn, wqr_ref[...]) * sq_ref[...]
    qb_ref[...] = q.astype(BF16)

    kvn = _rms(proj[:, 1792:1920], kvn_ref[...]).astype(BF16)
    t = proj[:, 1920:2048] * rk_ref[...]
    k_pe = t + pltpu.roll(t, 96, 1)
    k = _dot(kvn, wk_ref[...]) + _dot(k_pe.astype(BF16), place_ref[...])
    kb_ref[...] = k.astype(BF16)
    vb_ref[...] = _dot(kvn, wv_ref[...]).astype(BF16)


def _front_even(x, g, win, qnorm, kvnorm, wqa, wqr, wk, wv, place, cq, sq, rk, batch):
    n = x.shape[0]
    tm = TOKEN_TILE
    pt = SEQ // tm
    row = lambda p, b: (b * pt + p, 0)
    pos = lambda p, b: (p, 0)
    tok = lambda w: pl.BlockSpec((tm, w), row)
    out = lambda w: jax.ShapeDtypeStruct((n, w), BF16)
    return pl.pallas_call(
        _front_even_kernel,
        grid=(pt, batch),
        in_specs=[tok(D_MODEL), _resident((1, D_MODEL)), _resident(win.shape), _resident((1, B_Q_LORA)),
                  _resident((1, B_KV_LORA)), _resident(wqa.shape), _resident(wqr.shape), _resident(wk.shape),
                  _resident(wv.shape), _resident(place.shape),
                  pl.BlockSpec((tm, 1024), pos), pl.BlockSpec((tm, 1024), pos), pl.BlockSpec((tm, 128), pos)],
        out_specs=[tok(512), tok(512), tok(512), tok(1024), tok(1024), tok(512)],
        out_shape=[out(512), out(512), out(512), out(1024), out(1024), out(512)],
        compiler_params=_cparams("arbitrary", "arbitrary"),
        name="front_even",
    )(x, g, win, qnorm, kvnorm, wqa, wqr, wk, wv, place, cq, sq, rk)


A_TILES = SEQ // Q_TILE
A_MAX_DISTANCE = max(window // 2 for window, _ in A_PATTERNS)
A_TILE_REACH = (A_MAX_DISTANCE - 1) // Q_TILE + 1


def _head_lane_mask(shape):
    return lax.broadcasted_iota(jnp.int32, shape, 1) < HEAD_DIM


def _fill_value_with_ones(v_ref, vaug_ref):
    v = v_ref[...]
    lo = _head_lane_mask(v.shape)
    one = jnp.ones_like(v)
    vaug_ref[0] = jnp.where(lo, v, one)
    vaug_ref[1] = jnp.where(lo, one, v)


def _normalised_pair(acc0, acc1):
    lo = _head_lane_mask(acc0.shape)
    return jnp.where(lo, acc0 / pltpu.roll(acc0, HEAD_DIM, 1), acc1 / pltpu.roll(acc1, HEAD_DIM, 1))


def _attn_a_kernel(q_ref, k_ref, v_ref, bias_ref, o_ref, vaug_ref):
    _fill_value_with_ones(v_ref, vaug_ref)
    for i in range(A_TILES):
        rows = slice(i * Q_TILE, (i + 1) * Q_TILE)
        key_tiles = range(max(0, i - A_TILE_REACH), min(A_TILES, i + A_TILE_REACH + 1))
        q = q_ref[rows, :]
        lo_mask = _head_lane_mask(q.shape)
        accs = []
        for head in range(2):
            qm = jnp.where(lo_mask if head == 0 else jnp.logical_not(lo_mask), q, jnp.zeros_like(q))
            s = [_dot_nt(qm, k_ref[j * Q_TILE:(j + 1) * Q_TILE, :]) + bias_ref[head, j - i + A_TILE_REACH]
                 for j in key_tiles]
            m = functools.reduce(jnp.maximum, [jnp.max(t, axis=-1, keepdims=True) for t in s])
            accs.append(functools.reduce(jnp.add, [
                _dot(jnp.exp2(t - m).astype(BF16), vaug_ref[head, j * Q_TILE:(j + 1) * Q_TILE, :])
                for t, j in zip(s, key_tiles)]))
        o_ref[rows, :] = _normalised_pair(*accs).astype(o_ref.dtype)


def _attn_a(qa, ka, va, bias, batch):
    n = qa.shape[0]
    blk = pl.BlockSpec((SEQ, 2 * HEAD_DIM), lambda hp, b: (b, hp))
    return pl.pallas_call(
        _attn_a_kernel,
        grid=(A_HEADS // 2, batch),
        in_specs=[blk, blk, blk,
                  pl.BlockSpec((2, 2 * A_TILE_REACH + 1, Q_TILE, Q_TILE), lambda hp, b: (hp, 0, 0, 0))],
        out_specs=blk,
        out_shape=jax.ShapeDtypeStruct((n, A_WIDTH), BF16),
        scratch_shapes=[pltpu.VMEM((2, SEQ, 2 * HEAD_DIM), BF16)],
        compiler_params=_cparams("arbitrary", "arbitrary"),
        name="attn_dilated",
    )(qa, ka, va, bias)


def _dilated_bias():
    o = np.arange(-A_TILE_REACH, A_TILE_REACH + 1)[:, None, None]
    d = Q_TILE * o + np.arange(Q_TILE)[None, None, :] - np.arange(Q_TILE)[None, :, None]
    ad = np.abs(d)
    mult = np.zeros(d.shape, np.int32)
    for window, dilation in A_PATTERNS:
        mult += ((d % dilation == 0) & (ad <= window // 2)).astype(np.int32)
    logm = np.where(mult > 0, np.log(np.maximum(mult, 1).astype(np.float64)), 0.0).astype(np.float32)
    slopes = np.array([2.0 ** (-8.0 * (h + 1) / A_HEADS) for h in range(A_HEADS)], np.float32)
    bias = -jnp.asarray(slopes)[:, None, None, None] * jnp.asarray(ad.astype(np.float32))[None] + jnp.asarray(logm)[None]
    return jnp.where(jnp.asarray(mult > 0)[None], bias * LOG2E, NEG)


def _attn_mla_kernel(q_ref, k_ref, v_ref, o_ref, vaug_ref):
    _fill_value_with_ones(v_ref, vaug_ref)

    def q_tile(i, carry):
        q0 = pl.multiple_of(i * Q_TILE, Q_TILE)
        accs = []
        for head in range(2):
            lanes = slice(head * B_HEAD_LANES, (head + 1) * B_HEAD_LANES)
            s = _dot_nt(q_ref[pl.ds(q0, Q_TILE), lanes], k_ref[:, lanes])
            p = jnp.exp2(s - jnp.max(s, axis=-1, keepdims=True))
            accs.append(_dot(p.astype(BF16), vaug_ref[head]))
        o_ref[pl.ds(q0, Q_TILE), :] = _normalised_pair(*accs).astype(o_ref.dtype)
        return carry

    lax.fori_loop(0, SEQ // Q_TILE, q_tile, 0, unroll=4)


def _attn_mla(qb, kb, vb, batch):
    n = qb.shape[0]
    qk = pl.BlockSpec((SEQ, 2 * B_HEAD_LANES), lambda hp, b: (b, hp))
    vo = pl.BlockSpec((SEQ, 2 * B_V), lambda hp, b: (b, hp))
    return pl.pallas_call(
        _attn_mla_kernel,
        grid=(B_HEADS // 2, batch),
        in_specs=[qk, qk, vo],
        out_specs=vo,
        out_shape=jax.ShapeDtypeStruct((n, B_HEADS * B_V), BF16),
        scratch_shapes=[pltpu.VMEM((2, SEQ, 2 * B_V), BF16)],
        compiler_params=_cparams("arbitrary", "arbitrary"),
        name="attn_latent",
    )(qb, kb, vb)


def _post_even_kernel(x_ref, oa_ref, ob_ref, woa_ref, wob_ref, g_ref, wg_ref, wu_ref, wd_ref, o_ref):
    x1 = x_ref[...] + _dot(oa_ref[...], woa_ref[...]) + _dot(ob_ref[...], wob_ref[...])
    h = _rms(x1, g_ref[...]).astype(BF16)
    y = None
    for c in range(D_FF // FFN_CHUNK):
        cols = slice(c * FFN_CHUNK, (c + 1) * FFN_CHUNK)
        a = jax.nn.silu(_dot(h, wg_ref[:, cols])) * _dot(h, wu_ref[:, cols])
        d = _dot(a.astype(BF16), wd_ref[cols, :])
        y = d if y is None else y + d
    o_ref[...] = x1 + y


def _post_even(x, oa, ob, woa, wob, g, wg, wu, wd):
    n = x.shape[0]
    tm = TOKEN_TILE
    tok = lambda w: pl.BlockSpec((tm, w), lambda i: (i, 0))
    return pl.pallas_call(
        _post_even_kernel,
        grid=(n // tm,),
        in_specs=[tok(D_MODEL), tok(512), tok(512), _resident(woa.shape), _resident(wob.shape),
                  _resident((1, D_MODEL)), _resident(wg.shape), _resident(wu.shape), _resident(wd.shape)],
        out_specs=tok(D_MODEL),
        out_shape=jax.ShapeDtypeStruct((n, D_MODEL), F32),
        compiler_params=_cparams("arbitrary"),
        name="post_even",
    )(x, oa, ob, woa, wob, g, wg, wu, wd)


def _front_odd_kernel(x_ref, g_ref, w_ref, q_ref, k_ref, v_ref):
    h = _rms(x_ref[...], g_ref[...]).astype(BF16)
    qkv = _dot(h, w_ref[...])
    q_ref[...] = (qkv[:, 0:1024] * LOG2E).astype(BF16)
    k_ref[...] = qkv[:, 1024:2048].astype(BF16)
    v_ref[...] = qkv[:, 2048:3072].astype(BF16)


def _front_odd(x, g, w):
    n = x.shape[0]
    tm = TOKEN_TILE
    tok = pl.BlockSpec((tm, D_MODEL), lambda i: (i, 0))
    out = jax.ShapeDtypeStruct((n, D_MODEL), BF16)
    return pl.pallas_call(
        _front_odd_kernel,
        grid=(n // tm,),
        in_specs=[tok, _resident((1, D_MODEL)), _resident(w.shape)],
        out_specs=[tok, tok, tok],
        out_shape=[out, out, out],
        compiler_params=_cparams("arbitrary"),
        name="front_odd",
    )(x, g, w)


NA_Q = NA_Q_ROWS * GRID_W
NA_K = NA_K_ROWS * GRID_W
NA_BLOCKS = GRID_ROWS // NA_Q_ROWS


def _na_key_row_start(block):
    return np.clip(NA_Q_ROWS * block - NA_ROWS // 2, 0, GRID_ROWS - NA_K_ROWS)


def _attn_na_kernel(q_ref, k_ref, v_ref, bias_ref, o_ref, vaug_ref):
    _fill_value_with_ones(v_ref, vaug_ref)

    def q_block(bi, carry):
        q0 = pl.multiple_of(bi * NA_Q, NA_Q)
        ks = jnp.clip(NA_Q_ROWS * bi - NA_ROWS // 2, 0, GRID_ROWS - NA_K_ROWS)
        k0 = pl.multiple_of(ks * GRID_W, GRID_W)
        case = jnp.where(bi == 0, 0, jnp.where(bi == NA_BLOCKS - 1, 2, 1))
        q = q_ref[pl.ds(q0, NA_Q), :]
        k = k_ref[pl.ds(k0, NA_K), :]
        lo_mask = _head_lane_mask(q.shape)
        accs = []
        for head in range(2):
            qm = jnp.where(lo_mask if head == 0 else jnp.logical_not(lo_mask), q, jnp.zeros_like(q))
            s = _dot_nt(qm, k) + bias_ref[head, case]
            p = jnp.exp2(s - jnp.max(s, axis=-1, keepdims=True))
            accs.append(_dot(p.astype(BF16), vaug_ref[head, pl.ds(k0, NA_K), :]))
        o_ref[pl.ds(q0, NA_Q), :] = _normalised_pair(*accs).astype(o_ref.dtype)
        return carry

    lax.fori_loop(0, NA_BLOCKS, q_block, 0, unroll=True)


def _attn_na(q, k, v, bias, batch):
    n = q.shape[0]
    blk = pl.BlockSpec((SEQ, 2 * HEAD_DIM), lambda hp, b: (b, hp))
    return pl.pallas_call(
        _attn_na_kernel,
        grid=(C_HEADS // 2, batch),
        in_specs=[blk, blk, blk, pl.BlockSpec((2, 3, NA_Q, NA_K), lambda hp, b: (hp, 0, 0, 0))],
        out_specs=blk,
        out_shape=jax.ShapeDtypeStruct((n, C_HEADS * HEAD_DIM), BF16),
        scratch_shapes=[pltpu.VMEM((2, SEQ, 2 * HEAD_DIM), BF16)],
        compiler_params=_cparams("arbitrary", "arbitrary"),
        name="attn_neighbourhood",
    )(q, k, v, bias)


def _na_bias(rpb):
    row_sel, row_ok = [], []
    for block in (0, 1, NA_BLOCKS - 1):
        qi = NA_Q_ROWS * block + np.arange(NA_Q_ROWS)[:, None]
        kr = _na_key_row_start(block) + np.arange(NA_K_ROWS)[None, :]
        rs = np.clip(qi - NA_ROWS // 2, 0, GRID_ROWS - NA_ROWS)
        row_ok.append((kr >= rs) & (kr < rs + NA_ROWS))
        row_sel.append(np.eye(2 * NA_ROWS - 1, dtype=np.float32)[np.clip(kr - qi + NA_ROWS - 1, 0, 2 * NA_ROWS - 2)])
    qc = np.arange(GRID_W)[:, None]
    kc = np.arange(GRID_W)[None, :]
    cs = np.clip(qc - NA_COLS // 2, 0, GRID_W - NA_COLS)
    col_ok = (kc >= cs) & (kc < cs + NA_COLS)
    col_sel = np.eye(2 * NA_COLS - 1, dtype=np.float32)[np.clip(kc - qc + NA_COLS - 1, 0, 2 * NA_COLS - 2)]
    row_sel, row_ok = np.stack(row_sel), np.stack(row_ok)
    table = jnp.einsum("cqka,hab,xyb->hcqxky", jnp.asarray(row_sel), rpb.astype(F32), jnp.asarray(col_sel),
                       precision=lax.Precision.HIGHEST)
    valid = row_ok[:, :, None, :, None] & col_ok[None, None, :, None, :]
    return jnp.where(jnp.asarray(valid)[None], table * LOG2E, NEG).reshape(rpb.shape[0], 3, NA_Q, NA_K)


ROUTER_SPLIT = 2


def _post_odd_kernel(x_ref, att_ref, wo_ref, g_ref, wr_ref, x1_ref, xn_ref, idx_ref, gate_ref):
    rows_per_group = x_ref.shape[0] // ROUTER_SPLIT
    for group in range(ROUTER_SPLIT):
        rows = slice(group * rows_per_group, (group + 1) * rows_per_group)
        x1 = x_ref[rows, :] + _dot(att_ref[rows, :], wo_ref[...])
        x1_ref[rows, :] = x1
        xn = _rms(x1, g_ref[...])
        xn_ref[rows, :] = xn
        xh = xn.astype(BF16)
        xl = (xn - xh.astype(F32)).astype(BF16)
        both = _dot(xh, wr_ref[...])
        logits = (both[:, :128] + _dot(xl, wr_ref[:, :128])) + both[:, 128:]
        lane = lax.broadcasted_iota(jnp.int32, logits.shape, 1)
        logits = jnp.where(lane < N_EXPERTS, logits, -jnp.inf)
        m1 = jnp.max(logits, axis=-1, keepdims=True)
        i1 = jnp.min(jnp.where(logits == m1, lane, 128), axis=-1, keepdims=True)
        rest = jnp.where(lane == i1, -jnp.inf, logits)
        m2 = jnp.max(rest, axis=-1, keepdims=True)
        i2 = jnp.min(jnp.where(rest == m2, lane, 128), axis=-1, keepdims=True)
        e2 = jnp.exp(m2 - m1)
        w1 = 1.0 / (1.0 + e2)
        w2 = e2 / (1.0 + e2)
        idx_ref[rows, :] = jnp.where(lane == 0, i1, jnp.where(lane == 1, i2, 0))
        gate_ref[rows, :] = jnp.where(lane == 0, w1, jnp.where(lane == 1, w2, 0.0))


def _post_odd(x, att, wo, g, wr):
    wrh = wr.astype(BF16)
    wr = jnp.concatenate([wrh, (wr - wrh.astype(F32)).astype(BF16)], axis=-1)
    n = x.shape[0]
    tm = TOKEN_TILE
    tok = lambda w: pl.BlockSpec((tm, w), lambda i: (i, 0))
    return pl.pallas_call(
        _post_odd_kernel,
        grid=(n // tm,),
        in_specs=[tok(D_MODEL), tok(D_MODEL), _resident(wo.shape), _resident((1, D_MODEL)), _resident(wr.shape)],
        out_specs=[tok(D_MODEL), tok(D_MODEL), tok(128), tok(128)],
        out_shape=[jax.ShapeDtypeStruct((n, D_MODEL), F32), jax.ShapeDtypeStruct((n, D_MODEL), F32),
                   jax.ShapeDtypeStruct((n, 128), jnp.int32), jax.ShapeDtypeStruct((n, 128), F32)],
        compiler_params=_cparams("arbitrary"),
        name="post_odd_router",
    )(x, att, wo, g, wr)


def _moe_plan(expert_idx):
    n = expert_idx.shape[0]
    ts = MOE_ROW_TILE
    tiles = 2 * n // ts + N_EXPERTS
    flat = expert_idx.reshape(-1)
    onehot = (flat[:, None] == jnp.arange(N_EXPERTS, dtype=jnp.int32)[None]).astype(jnp.int32)
    before = jnp.cumsum(onehot, axis=0) - onehot
    counts = jnp.sum(onehot, axis=0)
    padded = (counts + ts - 1) // ts * ts
    ends = jnp.cumsum(padded)
    starts = ends - padded
    pos = jnp.sum(onehot * (starts[None] + before), axis=1)
    total = ends[-1]
    tile_start = jnp.arange(tiles, dtype=jnp.int32) * ts
    expert_at = lambda row: jnp.minimum(jnp.sum((row[..., None] >= ends).astype(jnp.int32), axis=-1), N_EXPERTS - 1)
    tile_expert = jnp.where(tile_start < total, expert_at(tile_start), expert_at(total - ts))
    regions = jnp.concatenate([ends, padded]).astype(jnp.int32)
    return (pos.astype(jnp.int32).reshape(n // MOE_TOKEN_TILE, 1, 2 * MOE_TOKEN_TILE),
            tile_expert.astype(jnp.int32), (total // ts).reshape(1).astype(jnp.int32), regions)


def _dispatch_kernel(reg_ref, pos_ref, x_ref, xs_hbm, zbuf, sem, zsem):
    tm, ts = MOE_TOKEN_TILE, MOE_ROW_TILE
    n_rows = xs_hbm.shape[0]

    @pl.when(pl.program_id(0) == 0)
    def _zero_fill():
        zbuf[...] = jnp.zeros_like(zbuf)
        total = reg_ref[N_EXPERTS - 1]
        zero_copy = lambda start: pltpu.make_async_copy(zbuf, xs_hbm.at[pl.ds(pl.multiple_of(start, ts), ts), :], zsem)
        for wait in (False, True):
            for e in range(N_EXPERTS):
                @pl.when(reg_ref[N_EXPERTS + e] > 0)
                def _(e=e, wait=wait):
                    copy = zero_copy(reg_ref[e] - ts)
                    copy.wait() if wait else copy.start()
                @pl.when(total + e * ts < n_rows)
                def _(e=e, wait=wait):
                    copy = zero_copy(total + e * ts)
                    copy.wait() if wait else copy.start()

    def row_copy(r, p):
        return pltpu.make_async_copy(x_ref.at[pl.ds(r, 1), :], xs_hbm.at[pl.ds(p, 1), :], sem)

    def start(r, c):
        row_copy(r, pos_ref[0, 0, 2 * r]).start(priority=0)
        row_copy(r, pos_ref[0, 0, 2 * r + 1]).start(priority=1)
        return c
    lax.fori_loop(0, tm, start, 0, unroll=True)

    def wait(r, c):
        row_copy(r, 0).wait()
        row_copy(r, 0).wait()
        return c
    lax.fori_loop(0, tm, wait, 0, unroll=8)


def _dispatch(xn, pos, regions):
    n = xn.shape[0]
    tm, ts = MOE_TOKEN_TILE, MOE_ROW_TILE
    grid_spec = pltpu.PrefetchScalarGridSpec(
        num_scalar_prefetch=1,
        grid=(n // tm,),
        in_specs=[pl.BlockSpec((1, 1, 2 * tm), lambda i, reg: (i, 0, 0), memory_space=pltpu.SMEM),
                  pl.BlockSpec((tm, D_MODEL), lambda i, reg: (i, 0))],
        out_specs=pl.BlockSpec(memory_space=pl.ANY),
        scratch_shapes=[pltpu.VMEM((ts, D_MODEL), F32), pltpu.SemaphoreType.DMA, pltpu.SemaphoreType.DMA],
    )
    return pl.pallas_call(
        _dispatch_kernel,
        grid_spec=grid_spec,
        out_shape=jax.ShapeDtypeStruct((2 * n + N_EXPERTS * ts, D_MODEL), F32),
        compiler_params=_cparams("arbitrary"),
        name="moe_dispatch",
    )(regions, pos, xn)


def _experts_kernel(te_ref, nt_ref, x_ref, wg_ref, wu_ref, wd_ref, o_ref):
    del te_ref

    @pl.when(pl.program_id(1) == 0)
    def _init():
        o_ref[...] = jnp.zeros_like(o_ref)

    @pl.when(pl.program_id(0) < nt_ref[0])
    def _compute():
        xb = x_ref[...].astype(BF16)
        a = jax.nn.silu(_dot(xb, wg_ref[0])) * _dot(xb, wu_ref[0])
        o_ref[...] += _dot(a.astype(BF16), wd_ref[0])


def _experts(xs, wg, wu, wd, tile_expert, n_tiles):
    ts, fc = MOE_ROW_TILE, MOE_F_CHUNK
    tiles = xs.shape[0] // ts
    nf = D_FF_EXPERT // fc
    row = lambda j, nt: jnp.minimum(j, nt[0] - 1)
    col = lambda j, f, nt: jnp.where(j < nt[0], f, nf - 1)
    grid_spec = pltpu.PrefetchScalarGridSpec(
        num_scalar_prefetch=2,
        grid=(tiles, nf),
        in_specs=[pl.BlockSpec((ts, D_MODEL), lambda j, f, te, nt: (row(j, nt), 0)),
                  pl.BlockSpec((1, D_MODEL, fc), lambda j, f, te, nt: (te[j], 0, col(j, f, nt))),
                  pl.BlockSpec((1, D_MODEL, fc), lambda j, f, te, nt: (te[j], 0, col(j, f, nt))),
                  pl.BlockSpec((1, fc, D_MODEL), lambda j, f, te, nt: (te[j], col(j, f, nt), 0))],
        out_specs=pl.BlockSpec((ts, D_MODEL), lambda j, f, te, nt: (j, 0)),
    )
    return pl.pallas_call(
        _experts_kernel,
        grid_spec=grid_spec,
        out_shape=jax.ShapeDtypeStruct(xs.shape, F32),
        compiler_params=_cparams("arbitrary", "arbitrary"),
        name="moe_experts",
    )(tile_expert, n_tiles, xs, wg, wu, wd)


def _combine_kernel(pos_ref, next_pos_ref, x_ref, gate_ref, g_ref, ys_hbm, o_ref, ybuf, sem, *, final_norm):
    i = pl.program_id(0)
    slot = i % 2

    def row_copy(s, k, r, p):
        return pltpu.make_async_copy(ys_hbm.at[pl.ds(p, 1), :], ybuf.at[s, k, pl.ds(r, 1), :], sem.at[s])

    def fetch(s, tile_pos_ref):
        def start(r, c):
            row_copy(s, 0, r, tile_pos_ref[0, 0, 2 * r]).start(priority=0)
            row_copy(s, 1, r, tile_pos_ref[0, 0, 2 * r + 1]).start(priority=1)
            return c
        lax.fori_loop(0, MOE_TOKEN_TILE, start, 0, unroll=True)

    @pl.when(i == 0)
    def _first():
        fetch(0, pos_ref)

    @pl.when(i + 1 < pl.num_programs(0))
    def _next():
        fetch(1 - slot, next_pos_ref)

    def wait(r, c):
        row_copy(slot, 0, r, 0).wait()
        row_copy(slot, 1, r, 0).wait()
        return c
    lax.fori_loop(0, MOE_TOKEN_TILE, wait, 0, unroll=8)

    gates = gate_ref[...]
    x = x_ref[...] + (gates[:, 0:1] * ybuf[slot, 0] + gates[:, 1:2] * ybuf[slot, 1])
    o_ref[...] = _rms(x, g_ref[...]) if final_norm else x


def _combine(x1, ys, pos, gates, g, final_norm):
    n = x1.shape[0]
    tm = MOE_TOKEN_TILE
    tok = lambda w: pl.BlockSpec((tm, w), lambda i: (i, 0))
    return pl.pallas_call(
        functools.partial(_combine_kernel, final_norm=final_norm),
        grid=(n // tm,),
        in_specs=[pl.BlockSpec((1, 1, 2 * tm), lambda i: (i, 0, 0), memory_space=pltpu.SMEM),
                  pl.BlockSpec((1, 1, 2 * tm), lambda i: (jnp.minimum(i + 1, n // tm - 1), 0, 0),
                               memory_space=pltpu.SMEM),
                  tok(D_MODEL), tok(128), _resident((1, D_MODEL)), pl.BlockSpec(memory_space=pl.ANY)],
        out_specs=tok(D_MODEL),
        out_shape=jax.ShapeDtypeStruct((n, D_MODEL), F32),
        scratch_shapes=[pltpu.VMEM((2, 2, tm, D_MODEL), F32), pltpu.SemaphoreType.DMA((2,))],
        compiler_params=_cparams("arbitrary"),
        name="moe_combine",
    )(pos, pos, x1, gates, g, ys)


def _rope_tables():
    half = B_ROPE // 2
    freqs = ROPE_THETA ** (-jnp.arange(half, dtype=F32) / half)
    ang = jnp.arange(SEQ, dtype=F32)[:, None] * freqs[None]
    cos = jnp.concatenate([jnp.cos(ang), jnp.cos(ang)], axis=-1)
    sin = jnp.concatenate([jnp.sin(ang), jnp.sin(ang)], axis=-1)
    scale = (B_NOPE + B_ROPE) ** -0.5 * LOG2E
    ones = jnp.ones((SEQ, B_NOPE), F32)
    zeros_n = jnp.zeros((SEQ, B_NOPE), F32)
    zeros_p = jnp.zeros((SEQ, B_HEAD_LANES - B_NOPE - B_ROPE), F32)
    cq = jnp.tile(jnp.concatenate([ones, cos, zeros_p], axis=-1), (1, B_HEADS)) * scale
    sq = jnp.tile(jnp.concatenate([zeros_n, sin, zeros_p], axis=-1), (1, B_HEADS)) * scale
    rk = jnp.concatenate([cos, sin, jnp.zeros((SEQ, 64), F32)], axis=-1)
    return cq, sq, rk


def _half_swap(w):
    half = B_ROPE // 2
    return jnp.concatenate([-w[..., half:], w[..., :half]], axis=-1)


def _prep_even(w_in, w_uq, w_ukv, w_out):
    scale_a = HEAD_DIM ** -0.5
    k_rope = w_in[:, 3 * A_WIDTH + B_Q_LORA + B_KV_LORA:]
    win = jnp.concatenate([w_in[:, :A_WIDTH] * scale_a, w_in[:, A_WIDTH:3 * A_WIDTH + B_Q_LORA + B_KV_LORA],
                           k_rope, _half_swap(k_rope), jnp.zeros((D_MODEL, 64), F32)], axis=-1).astype(BF16)
    uq = w_uq.reshape(B_Q_LORA, B_HEADS, B_NOPE + B_ROPE)
    pad = jnp.zeros((B_Q_LORA, B_HEADS, B_HEAD_LANES - B_NOPE - B_ROPE), F32)
    wqa = jnp.concatenate([uq, pad], axis=-1).reshape(B_Q_LORA, -1).astype(BF16)
    wqr = jnp.concatenate([jnp.zeros_like(uq[..., :B_NOPE]), _half_swap(uq[..., B_NOPE:]), pad], axis=-1)
    wqr = wqr.reshape(B_Q_LORA, -1).astype(BF16)
    ukv = w_ukv.reshape(B_KV_LORA, B_HEADS, B_NOPE + B_V)
    wk = jnp.concatenate([ukv[..., :B_NOPE], jnp.zeros((B_KV_LORA, B_HEADS, B_HEAD_LANES - B_NOPE), F32)], axis=-1)
    wk = wk.reshape(B_KV_LORA, -1).astype(BF16)
    wv = ukv[..., B_NOPE:].reshape(B_KV_LORA, -1).astype(BF16)
    place = np.zeros((128, B_HEADS, B_HEAD_LANES), np.float32)
    for r in range(B_ROPE):
        place[r, :, B_NOPE + r] = 1.0
    place = jnp.asarray(place.reshape(128, -1), BF16)
    wo = w_out.astype(BF16)
    return win, wqa, wqr, wk, wv, place, wo[:A_WIDTH], wo[A_WIDTH:]


def kernel(x, ln_mix_e, w_in_e, mla_q_norm, mla_w_uq, mla_kv_norm, mla_w_ukv, w_out_e, ln_ffn_e, ffn_w_gate,
           ffn_w_up, ffn_w_down, ln_mix_o, na_w_qkv, na_rpb, na_w_out, ln_ffn_o, moe_router, moe_w_gate,
           moe_w_up, moe_w_down, ln_final):
    batch, seq, d = x.shape
    assert (seq, d) == (SEQ, D_MODEL)
    n = batch * seq
    depth = ln_mix_e.shape[0] + ln_mix_o.shape[0]
    row = lambda v: v.reshape(1, -1).astype(F32)
    cq, sq, rk = _rope_tables()
    bias_a = _dilated_bias()
    xf = x.reshape(n, d)
    for layer in range(depth):
        i = layer // 2
        if layer % 2 == 0:
            win, wqa, wqr, wk, wv, place, woa, wob = _prep_even(w_in_e[i], mla_w_uq[i], mla_w_ukv[i], w_out_e[i])
            qa, ka, va, qb, kb, vb = _front_even(xf, row(ln_mix_e[i]), win, row(mla_q_norm[i]), row(mla_kv_norm[i]),
                                                 wqa, wqr, wk, wv, place, cq, sq, rk, batch)
            oa = _attn_a(qa, ka, va, bias_a, batch)
            ob = _attn_mla(qb, kb, vb, batch)
            xf = _post_even(xf, oa, ob, woa, wob, row(ln_ffn_e[i]), ffn_w_gate[i].astype(BF16),
                            ffn_w_up[i].astype(BF16), ffn_w_down[i].astype(BF16))
        else:
            wqkv = jnp.concatenate([na_w_qkv[i][:, :D_MODEL] * HEAD_DIM ** -0.5, na_w_qkv[i][:, D_MODEL:]], axis=-1)
            q, k, v = _front_odd(xf, row(ln_mix_o[i]), wqkv.astype(BF16))
            att = _attn_na(q, k, v, _na_bias(na_rpb[i]), batch)
            wr = jnp.concatenate([moe_router[i].astype(F32), jnp.zeros((D_MODEL, 128 - N_EXPERTS), F32)], axis=-1)
            x1, xn, idx, gates = _post_odd(xf, att, na_w_out[i].astype(BF16), row(ln_ffn_o[i]), wr)
            pos, tile_expert, n_tiles, regions = _moe_plan(idx[:, :2])
            xs = _dispatch(xn, pos, regions)
            ys = _experts(xs, moe_w_gate[i].astype(BF16), moe_w_up[i].astype(BF16), moe_w_down[i].astype(BF16),
                          tile_expert, n_tiles)
            xf = _combine(x1, ys, pos, gates, row(ln_final), final_norm=layer == depth - 1)
    if depth % 2 == 1:
        raise NotImplementedError("final norm is fused into the last odd layer")
    return xf.reshape(batch, seq, d)
```

```python
import functools
import math

import numpy as np
import jax
import jax.numpy as jnp
from jax import lax
from jax.experimental import pallas as pl
from jax.experimental.pallas import tpu as pltpu

F32 = jnp.float32
BF16 = jnp.bfloat16

D_MODEL = 1024
SEQ = 2048
HEAD_DIM = 64
NORM_EPS = 1e-6
NEG = -1e30
LOG2E = math.log2(math.e)

A_HEADS = 8
A_WIDTH = A_HEADS * HEAD_DIM
A_PATTERNS = ((128, 1), (512, 4), (2048, 16))

B_HEADS = 8
B_Q_LORA = 256
B_KV_LORA = 128
B_NOPE = 64
B_ROPE = 32
B_V = 64
ROPE_THETA = 10000.0
B_HEAD_LANES = 128

C_HEADS = 16
GRID_W = 64
GRID_ROWS = SEQ // GRID_W
NA_ROWS = 8
NA_COLS = 16

D_FF = 2816
N_EXPERTS = 8
D_FF_EXPERT = 3584

VMEM_LIMIT_BYTES = 56 * 1024 * 1024

TOKEN_TILE = 512
Q_TILE = 256
FFN_CHUNK = 1408
MOE_ROW_TILE = 1024
MOE_F_CHUNK = 512
MOE_TOKEN_TILE = 512
NA_Q_ROWS = 4
NA_K_ROWS = 12


def _cparams(*sem):
    return pltpu.CompilerParams(dimension_semantics=sem, vmem_limit_bytes=VMEM_LIMIT_BYTES)


def _rms(x, g):
    return x * lax.rsqrt(jnp.mean(x * x, axis=-1, keepdims=True) + NORM_EPS) * g


def _dot(a, b):
    return jnp.dot(a, b, preferred_element_type=F32)


def _dot_nt(a, b):
    return lax.dot_general(a, b, (((1,), (1,)), ((), ())), preferred_element_type=F32)


def _resident(shape):
    zeros = (0,) * len(shape)
    return pl.BlockSpec(shape, lambda *_: zeros, pipeline_mode=pl.Buffered(1))


FRONT_SPLIT = 1


def _front_even_kernel(x_ref, g_ref, win_ref, qn_ref, kvn_ref, wqa_ref, wqr_ref, wk_ref, wv_ref, place_ref,
                       cq_ref, sq_ref, rk_ref,
                       qa_ref, ka_ref, va_ref, qb_ref, kb_ref, vb_ref):
    rows_per_group = x_ref.shape[0] // FRONT_SPLIT
    for group in range(FRONT_SPLIT):
        rows = slice(group * rows_per_group, (group + 1) * rows_per_group)
        h = _rms(x_ref[rows, :], g_ref[...]).astype(BF16)
        proj = _dot(h, win_ref[...])
        qa_ref[rows, :] = (proj[:, 0:512] * LOG2E).astype(BF16)
        ka_ref[rows, :] = proj[:, 512:1024].astype(BF16)
        va_ref[rows, :] = proj[:, 1024:1536].astype(BF16)

        qn = _rms(proj[:, 1536:1792], qn_ref[...]).astype(BF16)
        q = _dot(qn, wqa_ref[...]) * cq_ref[rows, :] + _dot(qn, wqr_ref[...]) * sq_ref[rows, :]
        qb_ref[rows, :] = q.astype(BF16)

        kvn = _rms(proj[:, 1792:1920], kvn_ref[...]).astype(BF16)
        t = proj[:, 1920:2048] * rk_ref[rows, :]
        k_pe = t + pltpu.roll(t, 96, 1)
        k = _dot(kvn, wk_ref[...]) + _dot(k_pe.astype(BF16), place_ref[...])
        kb_ref[rows, :] = k.astype(BF16)
        vb_ref[rows, :] = _dot(kvn, wv_ref[...]).astype(BF16)


def _front_even(x, g, win, qnorm, kvnorm, wqa, wqr, wk, wv, place, cq, sq, rk, batch):
    n = x.shape[0]
    tm = TOKEN_TILE
    pt = SEQ // tm
    row = lambda p, b: (b * pt + p, 0)
    pos = lambda p, b: (p, 0)
    tok = lambda w: pl.BlockSpec((tm, w), row)
    out = lambda w: jax.ShapeDtypeStruct((n, w), BF16)
    return pl.pallas_call(
        _front_even_kernel,
        grid=(pt, batch),
        in_specs=[tok(D_MODEL), _resident((1, D_MODEL)), _resident(win.shape), _resident((1, B_Q_LORA)),
                  _resident((1, B_KV_LORA)), _resident(wqa.shape), _resident(wqr.shape), _resident(wk.shape),
                  _resident(wv.shape), _resident(place.shape),
                  pl.BlockSpec((tm, 1024), pos), pl.BlockSpec((tm, 1024), pos), pl.BlockSpec((tm, 128), pos)],
        out_specs=[tok(512), tok(512), tok(512), tok(1024), tok(1024), tok(512)],
        out_shape=[out(512), out(512), out(512), out(1024), out(1024), out(512)],
        compiler_params=_cparams("arbitrary", "arbitrary"),
        name="front_even",
    )(x, g, win, qnorm, kvnorm, wqa, wqr, wk, wv, place, cq, sq, rk)


A_TILES = SEQ // Q_TILE
A_MAX_DISTANCE = max(window // 2 for window, _ in A_PATTERNS)
A_TILE_REACH = (A_MAX_DISTANCE - 1) // Q_TILE + 1


def _head_lane_mask(shape):
    return lax.broadcasted_iota(jnp.int32, shape, 1) < HEAD_DIM


def _fill_value_with_ones(v_ref, vaug_ref):
    v = v_ref[...]
    lo = _head_lane_mask(v.shape)
    one = jnp.ones_like(v)
    vaug_ref[0] = jnp.where(lo, v, one)
    vaug_ref[1] = jnp.where(lo, one, v)


def _normalised_pair(acc0, acc1):
    lo = _head_lane_mask(acc0.shape)
    return jnp.where(lo, acc0 / pltpu.roll(acc0, HEAD_DIM, 1), acc1 / pltpu.roll(acc1, HEAD_DIM, 1))


def _attn_a_kernel(q_ref, k_ref, v_ref, bias_ref, o_ref, vaug_ref):
    _fill_value_with_ones(v_ref, vaug_ref)
    for i in range(A_TILES):
        rows = slice(i * Q_TILE, (i + 1) * Q_TILE)
        key_tiles = range(max(0, i - A_TILE_REACH), min(A_TILES, i + A_TILE_REACH + 1))
        q = q_ref[rows, :]
        lo_mask = _head_lane_mask(q.shape)
        accs = []
        for head in range(2):
            qm = jnp.where(lo_mask if head == 0 else jnp.logical_not(lo_mask), q, jnp.zeros_like(q))
            s = [_dot_nt(qm, k_ref[j * Q_TILE:(j + 1) * Q_TILE, :]) + bias_ref[head, j - i + A_TILE_REACH]
                 for j in key_tiles]
            m = functools.reduce(jnp.maximum, [jnp.max(t, axis=-1, keepdims=True) for t in s])
            accs.append(functools.reduce(jnp.add, [
                _dot(jnp.exp2(t - m).astype(BF16), vaug_ref[head, j * Q_TILE:(j + 1) * Q_TILE, :])
                for t, j in zip(s, key_tiles)]))
        o_ref[rows, :] = _normalised_pair(*accs).astype(o_ref.dtype)


def _attn_a(qa, ka, va, bias, batch):
    n = qa.shape[0]
    blk = pl.BlockSpec((SEQ, 2 * HEAD_DIM), lambda hp, b: (b, hp))
    return pl.pallas_call(
        _attn_a_kernel,
        grid=(A_HEADS // 2, batch),
        in_specs=[blk, blk, blk,
                  pl.BlockSpec((2, 2 * A_TILE_REACH + 1, Q_TILE, Q_TILE), lambda hp, b: (hp, 0, 0, 0))],
        out_specs=blk,
        out_shape=jax.ShapeDtypeStruct((n, A_WIDTH), BF16),
        scratch_shapes=[pltpu.VMEM((2, SEQ, 2 * HEAD_DIM), BF16)],
        compiler_params=_cparams("arbitrary", "arbitrary"),
        name="attn_dilated",
    )(qa, ka, va, bias)


def _dilated_bias():
    o = np.arange(-A_TILE_REACH, A_TILE_REACH + 1)[:, None, None]
    d = Q_TILE * o + np.arange(Q_TILE)[None, None, :] - np.arange(Q_TILE)[None, :, None]
    ad = np.abs(d)
    mult = np.zeros(d.shape, np.int32)
    for window, dilation in A_PATTERNS:
        mult += ((d % dilation == 0) & (ad <= window // 2)).astype(np.int32)
    logm = np.where(mult > 0, np.log(np.maximum(mult, 1).astype(np.float64)), 0.0).astype(np.float32)
    slopes = np.array([2.0 ** (-8.0 * (h + 1) / A_HEADS) for h in range(A_HEADS)], np.float32)
    bias = -jnp.asarray(slopes)[:, None, None, None] * jnp.asarray(ad.astype(np.float32))[None] + jnp.asarray(logm)[None]
    return jnp.where(jnp.asarray(mult > 0)[None], bias * LOG2E, NEG)


def _attn_mla_kernel(q_ref, k_ref, v_ref, o_ref, vaug_ref):
    _fill_value_with_ones(v_ref, vaug_ref)

    def q_tile(i, carry):
        q0 = pl.multiple_of(i * Q_TILE, Q_TILE)
        accs = []
        for head in range(2):
            lanes = slice(head * B_HEAD_LANES, (head + 1) * B_HEAD_LANES)
            s = _dot_nt(q_ref[pl.ds(q0, Q_TILE), lanes], k_ref[:, lanes])
            p = jnp.exp2(s - jnp.max(s, axis=-1, keepdims=True))
            accs.append(_dot(p.astype(BF16), vaug_ref[head]))
        o_ref[pl.ds(q0, Q_TILE), :] = _normalised_pair(*accs).astype(o_ref.dtype)
        return carry

    lax.fori_loop(0, SEQ // Q_TILE, q_tile, 0, unroll=4)


def _attn_mla(qb, kb, vb, batch):
    n = qb.shape[0]
    qk = pl.BlockSpec((SEQ, 2 * B_HEAD_LANES), lambda hp, b: (b, hp))
    vo = pl.BlockSpec((SEQ, 2 * B_V), lambda hp, b: (b, hp))
    return pl.pallas_call(
        _attn_mla_kernel,
        grid=(B_HEADS // 2, batch),
        in_specs=[qk, qk, vo],
        out_specs=vo,
        out_shape=jax.ShapeDtypeStruct((n, B_HEADS * B_V), BF16),
        scratch_shapes=[pltpu.VMEM((2, SEQ, 2 * B_V), BF16)],
        compiler_params=_cparams("arbitrary", "arbitrary"),
        name="attn_latent",
    )(qb, kb, vb)


def _post_even_kernel(x_ref, oa_ref, ob_ref, woa_ref, wob_ref, g_ref, wg_ref, wu_ref, wd_ref, o_ref):
    x1 = x_ref[...] + _dot(oa_ref[...], woa_ref[...]) + _dot(ob_ref[...], wob_ref[...])
    h = _rms(x1, g_ref[...]).astype(BF16)
    y = None
    for c in range(D_FF // FFN_CHUNK):
        cols = slice(c * FFN_CHUNK, (c + 1) * FFN_CHUNK)
        a = jax.nn.silu(_dot(h, wg_ref[:, cols])) * _dot(h, wu_ref[:, cols])
        d = _dot(a.astype(BF16), wd_ref[cols, :])
        y = d if y is None else y + d
    o_ref[...] = x1 + y


def _post_even(x, oa, ob, woa, wob, g, wg, wu, wd):
    n = x.shape[0]
    tm = TOKEN_TILE
    tok = lambda w: pl.BlockSpec((tm, w), lambda i: (i, 0))
    return pl.pallas_call(
        _post_even_kernel,
        grid=(n // tm,),
        in_specs=[tok(D_MODEL), tok(512), tok(512), _resident(woa.shape), _resident(wob.shape),
                  _resident((1, D_MODEL)), _resident(wg.shape), _resident(wu.shape), _resident(wd.shape)],
        out_specs=tok(D_MODEL),
        out_shape=jax.ShapeDtypeStruct((n, D_MODEL), F32),
        compiler_params=_cparams("arbitrary"),
        name="post_even",
    )(x, oa, ob, woa, wob, g, wg, wu, wd)


def _front_odd_kernel(x_ref, g_ref, w_ref, q_ref, k_ref, v_ref):
    h = _rms(x_ref[...], g_ref[...]).astype(BF16)
    qkv = _dot(h, w_ref[...])
    q_ref[...] = (qkv[:, 0:1024] * LOG2E).astype(BF16)
    k_ref[...] = qkv[:, 1024:2048].astype(BF16)
    v_ref[...] = qkv[:, 2048:3072].astype(BF16)


def _front_odd(x, g, w):
    n = x.shape[0]
    tm = TOKEN_TILE
    tok = pl.BlockSpec((tm, D_MODEL), lambda i: (i, 0))
    out = jax.ShapeDtypeStruct((n, D_MODEL), BF16)
    return pl.pallas_call(
        _front_odd_kernel,
        grid=(n // tm,),
        in_specs=[tok, _resident((1, D_MODEL)), _resident(w.shape)],
        out_specs=[tok, tok, tok],
        out_shape=[out, out, out],
        compiler_params=_cparams("arbitrary"),
        name="front_odd",
    )(x, g, w)


NA_Q = NA_Q_ROWS * GRID_W
NA_K = NA_K_ROWS * GRID_W
NA_BLOCKS = GRID_ROWS // NA_Q_ROWS


def _na_key_rows(block):
    if block == 0:
        return 0, 0, NA_ROWS
    if block == NA_BLOCKS - 1:
        return 2, GRID_ROWS - NA_ROWS, NA_ROWS
    return 1, NA_Q_ROWS * block - NA_ROWS // 2, NA_K_ROWS


def _attn_na_kernel(q_ref, k_ref, v_ref, bias_ref, o_ref, vaug_ref):
    _fill_value_with_ones(v_ref, vaug_ref)
    for block in range(NA_BLOCKS):
        case, key_row, key_rows = _na_key_rows(block)
        rows = slice(block * NA_Q, (block + 1) * NA_Q)
        keys = slice(key_row * GRID_W, (key_row + key_rows) * GRID_W)
        q = q_ref[rows, :]
        k = k_ref[keys, :]
        lo_mask = _head_lane_mask(q.shape)
        accs = []
        for head in range(2):
            qm = jnp.where(lo_mask if head == 0 else jnp.logical_not(lo_mask), q, jnp.zeros_like(q))
            s = _dot_nt(qm, k) + bias_ref[head, case, :, :key_rows * GRID_W]
            p = jnp.exp2(s - jnp.max(s, axis=-1, keepdims=True))
            accs.append(_dot(p.astype(BF16), vaug_ref[head, keys, :]))
        o_ref[rows, :] = _normalised_pair(*accs).astype(o_ref.dtype)


def _attn_na(q, k, v, bias, batch):
    n = q.shape[0]
    blk = pl.BlockSpec((SEQ, 2 * HEAD_DIM), lambda hp, b: (b, hp))
    return pl.pallas_call(
        _attn_na_kernel,
        grid=(C_HEADS // 2, batch),
        in_specs=[blk, blk, blk, pl.BlockSpec((2, 3, NA_Q, NA_K), lambda hp, b: (hp, 0, 0, 0))],
        out_specs=blk,
        out_shape=jax.ShapeDtypeStruct((n, C_HEADS * HEAD_DIM), BF16),
        scratch_shapes=[pltpu.VMEM((2, SEQ, 2 * HEAD_DIM), BF16)],
        compiler_params=_cparams("arbitrary", "arbitrary"),
        name="attn_neighbourhood",
    )(q, k, v, bias)


def _na_bias(rpb):
    row_sel, row_ok = [], []
    for block in (0, 1, NA_BLOCKS - 1):
        _, key_row, key_rows = _na_key_rows(block)
        qi = NA_Q_ROWS * block + np.arange(NA_Q_ROWS)[:, None]
        kr = key_row + np.arange(NA_K_ROWS)[None, :]
        rs = np.clip(qi - NA_ROWS // 2, 0, GRID_ROWS - NA_ROWS)
        row_ok.append((kr >= rs) & (kr < rs + NA_ROWS) & (kr < key_row + key_rows))
        row_sel.append(np.eye(2 * NA_ROWS - 1, dtype=np.float32)[np.clip(kr - qi + NA_ROWS - 1, 0, 2 * NA_ROWS - 2)])
    qc = np.arange(GRID_W)[:, None]
    kc = np.arange(GRID_W)[None, :]
    cs = np.clip(qc - NA_COLS // 2, 0, GRID_W - NA_COLS)
    col_ok = (kc >= cs) & (kc < cs + NA_COLS)
    col_sel = np.eye(2 * NA_COLS - 1, dtype=np.float32)[np.clip(kc - qc + NA_COLS - 1, 0, 2 * NA_COLS - 2)]
    row_sel, row_ok = np.stack(row_sel), np.stack(row_ok)
    table = jnp.einsum("cqka,hab,xyb->hcqxky", jnp.asarray(row_sel), rpb.astype(F32), jnp.asarray(col_sel),
                       precision=lax.Precision.HIGHEST)
    valid = row_ok[:, :, None, :, None] & col_ok[None, None, :, None, :]
    return jnp.where(jnp.asarray(valid)[None], table * LOG2E, NEG).reshape(rpb.shape[0], 3, NA_Q, NA_K)


ROUTER_SPLIT = 2


def _post_odd_kernel(x_ref, att_ref, wo_ref, g_ref, wr_ref, x1_ref, xn_ref, idx_ref, gate_ref):
    rows_per_group = x_ref.shape[0] // ROUTER_SPLIT
    for group in range(ROUTER_SPLIT):
        rows = slice(group * rows_per_group, (group + 1) * rows_per_group)
        x1 = x_ref[rows, :] + _dot(att_ref[rows, :], wo_ref[...])
        x1_ref[rows, :] = x1
        xn = _rms(x1, g_ref[...])
        xn_ref[rows, :] = xn
        xh = xn.astype(BF16)
        xl = (xn - xh.astype(F32)).astype(BF16)
        both = _dot(xh, wr_ref[...])
        logits = (both[:, :128] + _dot(xl, wr_ref[:, :128])) + both[:, 128:]
        lane = lax.broadcasted_iota(jnp.int32, logits.shape, 1)
        logits = jnp.where(lane < N_EXPERTS, logits, -jnp.inf)
        m1 = jnp.max(logits, axis=-1, keepdims=True)
        i1 = jnp.min(jnp.where(logits == m1, lane, 128), axis=-1, keepdims=True)
        rest = jnp.where(lane == i1, -jnp.inf, logits)
        m2 = jnp.max(rest, axis=-1, keepdims=True)
        i2 = jnp.min(jnp.where(rest == m2, lane, 128), axis=-1, keepdims=True)
        e2 = jnp.exp(m2 - m1)
        w1 = 1.0 / (1.0 + e2)
        w2 = e2 / (1.0 + e2)
        idx_ref[rows, :] = jnp.where(lane == 0, i1, jnp.where(lane == 1, i2, 0))
        gate_ref[rows, :] = jnp.where(lane == 0, w1, jnp.where(lane == 1, w2, 0.0))


def _post_odd(x, att, wo, g, wr):
    wrh = wr.astype(BF16)
    wr = jnp.concatenate([wrh, (wr - wrh.astype(F32)).astype(BF16)], axis=-1)
    n = x.shape[0]
    tm = TOKEN_TILE
    tok = lambda w: pl.BlockSpec((tm, w), lambda i: (i, 0))
    return pl.pallas_call(
        _post_odd_kernel,
        grid=(n // tm,),
        in_specs=[tok(D_MODEL), tok(D_MODEL), _resident(wo.shape), _resident((1, D_MODEL)), _resident(wr.shape)],
        out_specs=[tok(D_MODEL), tok(D_MODEL), tok(128), tok(128)],
        out_shape=[jax.ShapeDtypeStruct((n, D_MODEL), F32), jax.ShapeDtypeStruct((n, D_MODEL), F32),
                   jax.ShapeDtypeStruct((n, 128), jnp.int32), jax.ShapeDtypeStruct((n, 128), F32)],
        compiler_params=_cparams("arbitrary"),
        name="post_odd_router",
    )(x, att, wo, g, wr)


def _moe_plan(expert_idx):
    n = expert_idx.shape[0]
    ts = MOE_ROW_TILE
    tiles = 2 * n // ts + N_EXPERTS
    flat = expert_idx.reshape(-1)
    onehot = (flat[:, None] == jnp.arange(N_EXPERTS, dtype=jnp.int32)[None]).astype(jnp.int32)
    before = jnp.cumsum(onehot, axis=0) - onehot
    counts = jnp.sum(onehot, axis=0)
    padded = (counts + ts - 1) // ts * ts
    ends = jnp.cumsum(padded)
    starts = ends - padded
    pos = jnp.sum(onehot * (starts[None] + before), axis=1)
    total = ends[-1]
    tile_start = jnp.arange(tiles, dtype=jnp.int32) * ts
    expert_at = lambda row: jnp.minimum(jnp.sum((row[..., None] >= ends).astype(jnp.int32), axis=-1), N_EXPERTS - 1)
    tile_expert = jnp.where(tile_start < total, expert_at(tile_start), expert_at(total - ts))
    regions = jnp.concatenate([ends, padded]).astype(jnp.int32)
    return (pos.astype(jnp.int32).reshape(n // MOE_TOKEN_TILE, 1, 2 * MOE_TOKEN_TILE),
            tile_expert.astype(jnp.int32), (total // ts).reshape(1).astype(jnp.int32), regions)


def _dispatch_kernel(reg_ref, pos_ref, x_ref, xs_hbm, zbuf, sem, zsem):
    tm, ts = MOE_TOKEN_TILE, MOE_ROW_TILE
    n_rows = xs_hbm.shape[0]

    @pl.when(pl.program_id(0) == 0)
    def _zero_fill():
        zbuf[...] = jnp.zeros_like(zbuf)
        total = reg_ref[N_EXPERTS - 1]
        zero_copy = lambda start: pltpu.make_async_copy(zbuf, xs_hbm.at[pl.ds(pl.multiple_of(start, ts), ts), :], zsem)
        for wait in (False, True):
            for e in range(N_EXPERTS):
                @pl.when(reg_ref[N_EXPERTS + e] > 0)
                def _(e=e, wait=wait):
                    copy = zero_copy(reg_ref[e] - ts)
                    copy.wait() if wait else copy.start()
                @pl.when(total + e * ts < n_rows)
                def _(e=e, wait=wait):
                    copy = zero_copy(total + e * ts)
                    copy.wait() if wait else copy.start()

    def row_copy(r, p):
        return pltpu.make_async_copy(x_ref.at[pl.ds(r, 1), :], xs_hbm.at[pl.ds(p, 1), :], sem)

    def start(r, c):
        row_copy(r, pos_ref[0, 0, 2 * r]).start(priority=0)
        row_copy(r, pos_ref[0, 0, 2 * r + 1]).start(priority=1)
        return c
    lax.fori_loop(0, tm, start, 0, unroll=True)

    def wait(r, c):
        row_copy(r, 0).wait()
        row_copy(r, 0).wait()
        return c
    lax.fori_loop(0, tm, wait, 0, unroll=8)


def _dispatch(xn, pos, regions):
    n = xn.shape[0]
    tm, ts = MOE_TOKEN_TILE, MOE_ROW_TILE
    grid_spec = pltpu.PrefetchScalarGridSpec(
        num_scalar_prefetch=1,
        grid=(n // tm,),
        in_specs=[pl.BlockSpec((1, 1, 2 * tm), lambda i, reg: (i, 0, 0), memory_space=pltpu.SMEM),
                  pl.BlockSpec((tm, D_MODEL), lambda i, reg: (i, 0))],
        out_specs=pl.BlockSpec(memory_space=pl.ANY),
        scratch_shapes=[pltpu.VMEM((ts, D_MODEL), F32), pltpu.SemaphoreType.DMA, pltpu.SemaphoreType.DMA],
    )
    return pl.pallas_call(
        _dispatch_kernel,
        grid_spec=grid_spec,
        out_shape=jax.ShapeDtypeStruct((2 * n + N_EXPERTS * ts, D_MODEL), F32),
        compiler_params=_cparams("arbitrary"),
        name="moe_dispatch",
    )(regions, pos, xn)


def _experts_kernel(te_ref, nt_ref, x_ref, wg_ref, wu_ref, wd_ref, o_ref, xb_ref):
    del te_ref

    @pl.when(pl.program_id(1) == 0)
    def _init():
        o_ref[...] = jnp.zeros_like(o_ref)
        xb_ref[...] = x_ref[...].astype(BF16)

    @pl.when(pl.program_id(0) < nt_ref[0])
    def _compute():
        xb = xb_ref[...]
        a = jax.nn.silu(_dot(xb, wg_ref[0])) * _dot(xb, wu_ref[0])
        o_ref[...] += _dot(a.astype(BF16), wd_ref[0])


def _experts(xs, wg, wu, wd, tile_expert, n_tiles):
    ts, fc = MOE_ROW_TILE, MOE_F_CHUNK
    tiles = xs.shape[0] // ts
    nf = D_FF_EXPERT // fc
    row = lambda j, nt: jnp.minimum(j, nt[0] - 1)
    col = lambda j, f, nt: jnp.where(j < nt[0], f, nf - 1)
    grid_spec = pltpu.PrefetchScalarGridSpec(
        num_scalar_prefetch=2,
        grid=(tiles, nf),
        in_specs=[pl.BlockSpec((ts, D_MODEL), lambda j, f, te, nt: (row(j, nt), 0)),
                  pl.BlockSpec((1, D_MODEL, fc), lambda j, f, te, nt: (te[j], 0, col(j, f, nt))),
                  pl.BlockSpec((1, D_MODEL, fc), lambda j, f, te, nt: (te[j], 0, col(j, f, nt))),
                  pl.BlockSpec((1, fc, D_MODEL), lambda j, f, te, nt: (te[j], col(j, f, nt), 0))],
        out_specs=pl.BlockSpec((ts, D_MODEL), lambda j, f, te, nt: (j, 0)),
        scratch_shapes=[pltpu.VMEM((ts, D_MODEL), BF16)],
    )
    return pl.pallas_call(
        _experts_kernel,
        grid_spec=grid_spec,
        out_shape=jax.ShapeDtypeStruct(xs.shape, F32),
        compiler_params=_cparams("arbitrary", "arbitrary"),
        name="moe_experts",
    )(tile_expert, n_tiles, xs, wg, wu, wd)


def _combine_kernel(pos_ref, next_pos_ref, x_ref, gate_ref, g_ref, ys_hbm, o_ref, ybuf, sem, *, final_norm):
    i = pl.program_id(0)
    slot = i % 2

    def row_copy(s, k, r, p):
        return pltpu.make_async_copy(ys_hbm.at[pl.ds(p, 1), :], ybuf.at[s, k, pl.ds(r, 1), :], sem.at[s])

    def fetch(s, tile_pos_ref):
        def start(r, c):
            row_copy(s, 0, r, tile_pos_ref[0, 0, 2 * r]).start(priority=0)
            row_copy(s, 1, r, tile_pos_ref[0, 0, 2 * r + 1]).start(priority=1)
            return c
        lax.fori_loop(0, MOE_TOKEN_TILE, start, 0, unroll=True)

    @pl.when(i == 0)
    def _first():
        fetch(0, pos_ref)

    @pl.when(i + 1 < pl.num_programs(0))
    def _next():
        fetch(1 - slot, next_pos_ref)

    def wait(r, c):
        row_copy(slot, 0, r, 0).wait()
        row_copy(slot, 1, r, 0).wait()
        return c
    lax.fori_loop(0, MOE_TOKEN_TILE, wait, 0, unroll=8)

    gates = gate_ref[...]
    x = x_ref[...] + (gates[:, 0:1] * ybuf[slot, 0] + gates[:, 1:2] * ybuf[slot, 1])
    o_ref[...] = _rms(x, g_ref[...]) if final_norm else x


def _combine(x1, ys, pos, gates, g, final_norm):
    n = x1.shape[0]
    tm = MOE_TOKEN_TILE
    tok = lambda w: pl.BlockSpec((tm, w), lambda i: (i, 0))
    return pl.pallas_call(
        functools.partial(_combine_kernel, final_norm=final_norm),
        grid=(n // tm,),
        in_specs=[pl.BlockSpec((1, 1, 2 * tm), lambda i: (i, 0, 0), memory_space=pltpu.SMEM),
                  pl.BlockSpec((1, 1, 2 * tm), lambda i: (jnp.minimum(i + 1, n // tm - 1), 0, 0),
                               memory_space=pltpu.SMEM),
                  tok(D_MODEL), tok(128), _resident((1, D_MODEL)), pl.BlockSpec(memory_space=pl.ANY)],
        out_specs=tok(D_MODEL),
        out_shape=jax.ShapeDtypeStruct((n, D_MODEL), F32),
        scratch_shapes=[pltpu.VMEM((2, 2, tm, D_MODEL), F32), pltpu.SemaphoreType.DMA((2,))],
        compiler_params=_cparams("arbitrary"),
        name="moe_combine",
    )(pos, pos, x1, gates, g, ys)


def _rope_tables():
    half = B_ROPE // 2
    freqs = ROPE_THETA ** (-jnp.arange(half, dtype=F32) / half)
    ang = jnp.arange(SEQ, dtype=F32)[:, None] * freqs[None]
    cos = jnp.concatenate([jnp.cos(ang), jnp.cos(ang)], axis=-1)
    sin = jnp.concatenate([jnp.sin(ang), jnp.sin(ang)], axis=-1)
    scale = (B_NOPE + B_ROPE) ** -0.5 * LOG2E
    ones = jnp.ones((SEQ, B_NOPE), F32)
    zeros_n = jnp.zeros((SEQ, B_NOPE), F32)
    zeros_p = jnp.zeros((SEQ, B_HEAD_LANES - B_NOPE - B_ROPE), F32)
    cq = jnp.tile(jnp.concatenate([ones, cos, zeros_p], axis=-1), (1, B_HEADS)) * scale
    sq = jnp.tile(jnp.concatenate([zeros_n, sin, zeros_p], axis=-1), (1, B_HEADS)) * scale
    rk = jnp.concatenate([cos, sin, jnp.zeros((SEQ, 64), F32)], axis=-1)
    return cq, sq, rk


def _half_swap(w):
    half = B_ROPE // 2
    return jnp.concatenate([-w[..., half:], w[..., :half]], axis=-1)


def _prep_even(w_in, w_uq, w_ukv, w_out):
    scale_a = HEAD_DIM ** -0.5
    k_rope = w_in[:, 3 * A_WIDTH + B_Q_LORA + B_KV_LORA:]
    win = jnp.concatenate([w_in[:, :A_WIDTH] * scale_a, w_in[:, A_WIDTH:3 * A_WIDTH + B_Q_LORA + B_KV_LORA],
                           k_rope, _half_swap(k_rope), jnp.zeros((D_MODEL, 64), F32)], axis=-1).astype(BF16)
    uq = w_uq.reshape(B_Q_LORA, B_HEADS, B_NOPE + B_ROPE)
    pad = jnp.zeros((B_Q_LORA, B_HEADS, B_HEAD_LANES - B_NOPE - B_ROPE), F32)
    wqa = jnp.concatenate([uq, pad], axis=-1).reshape(B_Q_LORA, -1).astype(BF16)
    wqr = jnp.concatenate([jnp.zeros_like(uq[..., :B_NOPE]), _half_swap(uq[..., B_NOPE:]), pad], axis=-1)
    wqr = wqr.reshape(B_Q_LORA, -1).astype(BF16)
    ukv = w_ukv.reshape(B_KV_LORA, B_HEADS, B_NOPE + B_V)
    wk = jnp.concatenate([ukv[..., :B_NOPE], jnp.zeros((B_KV_LORA, B_HEADS, B_HEAD_LANES - B_NOPE), F32)], axis=-1)
    wk = wk.reshape(B_KV_LORA, -1).astype(BF16)
    wv = ukv[..., B_NOPE:].reshape(B_KV_LORA, -1).astype(BF16)
    place = np.zeros((128, B_HEADS, B_HEAD_LANES), np.float32)
    for r in range(B_ROPE):
        place[r, :, B_NOPE + r] = 1.0
    place = jnp.asarray(place.reshape(128, -1), BF16)
    wo = w_out.astype(BF16)
    return win, wqa, wqr, wk, wv, place, wo[:A_WIDTH], wo[A_WIDTH:]


def kernel(x, ln_mix_e, w_in_e, mla_q_norm, mla_w_uq, mla_kv_norm, mla_w_ukv, w_out_e, ln_ffn_e, ffn_w_gate,
           ffn_w_up, ffn_w_down, ln_mix_o, na_w_qkv, na_rpb, na_w_out, ln_ffn_o, moe_router, moe_w_gate,
           moe_w_up, moe_w_down, ln_final):
    batch, seq, d = x.shape
    assert (seq, d) == (SEQ, D_MODEL)
    n = batch * seq
    depth = ln_mix_e.shape[0] + ln_mix_o.shape[0]
    row = lambda v: v.reshape(1, -1).astype(F32)
    cq, sq, rk = _rope_tables()
    bias_a = _dilated_bias()
    xf = x.reshape(n, d)
    for layer in range(depth):
        i = layer // 2
        if layer % 2 == 0:
            win, wqa, wqr, wk, wv, place, woa, wob = _prep_even(w_in_e[i], mla_w_uq[i], mla_w_ukv[i], w_out_e[i])
            qa, ka, va, qb, kb, vb = _front_even(xf, row(ln_mix_e[i]), win, row(mla_q_norm[i]), row(mla_kv_norm[i]),
                                                 wqa, wqr, wk, wv, place, cq, sq, rk, batch)
            oa = _attn_a(qa, ka, va, bias_a, batch)
            ob = _attn_mla(qb, kb, vb, batch)
            xf = _post_even(xf, oa, ob, woa, wob, row(ln_ffn_e[i]), ffn_w_gate[i].astype(BF16),
                            ffn_w_up[i].astype(BF16), ffn_w_down[i].astype(BF16))
        else:
            wqkv = jnp.concatenate([na_w_qkv[i][:, :D_MODEL] * HEAD_DIM ** -0.5, na_w_qkv[i][:, D_MODEL:]], axis=-1)
            q, k, v = _front_odd(xf, row(ln_mix_o[i]), wqkv.astype(BF16))
            att = _attn_na(q, k, v, _na_bias(na_rpb[i]), batch)
            wr = jnp.concatenate([moe_router[i].astype(F32), jnp.zeros((D_MODEL, 128 - N_EXPERTS), F32)], axis=-1)
            x1, xn, idx, gates = _post_odd(xf, att, na_w_out[i].astype(BF16), row(ln_ffn_o[i]), wr)
            pos, tile_expert, n_tiles, regions = _moe_plan(idx[:, :2])
            xs = _dispatch(xn, pos, regions)
            ys = _experts(xs, moe_w_gate[i].astype(BF16), moe_w_up[i].astype(BF16), moe_w_down[i].astype(BF16),
                          tile_expert, n_tiles)
            xf = _combine(x1, ys, pos, gates, row(ln_final), final_norm=layer == depth - 1)
    if depth % 2 == 1:
        raise NotImplementedError("final norm is fused into the last odd layer")
    return xf.reshape(batch, seq, d)
```
